```python
import jax
import jax.numpy as jnp
from jax import lax
import numpy as np

D_MODEL = 1024
BATCH = 4
SEQ = 4096
DEPTH = 2
DEC_BATCH = 16
DEC_SEQ = 64
PAST_LEN = 2048

CHUNK = 64
N_META = 16
RMS_EPS = 1e-6
N_HEADS = 8
HEAD_DIM = 64
ATTN_WIDTH = N_HEADS * HEAD_DIM
ROT_DIM = HEAD_DIM // 4
ROPE_THETA = 500000.0
IDX_HEADS = 4
IDX_DIM = 64
TOPK_MAX = 256
QBLOCK = 128
PAD_CHUNK = 2 ** 30
LRU_WIDTH = 512
LRU_BLOCKS = 8
LRU_BLOCK_DIM = LRU_WIDTH // LRU_BLOCKS
CONV_WIDTH = 4
LRU_C = 8.0
POOL_WIDTH = 512
POOL_WINDOWS = (2, 4, 8, 16)
POOL_GROUPS = 4
POOL_GROUP_DIM = POOL_WIDTH // POOL_GROUPS
POOL_HIST = 15
N_BRANCH = 3
BRANCH_WIDTH = 512
COL_SIZES = (ATTN_WIDTH, ATTN_WIDTH, ATTN_WIDTH, ATTN_WIDTH, IDX_HEADS * IDX_DIM, IDX_DIM, IDX_HEADS,
             LRU_WIDTH, LRU_WIDTH, POOL_WIDTH, POOL_WIDTH, N_BRANCH * D_MODEL)
N_IN = 4 * ATTN_WIDTH + IDX_HEADS * IDX_DIM + IDX_DIM + IDX_HEADS + 2 * LRU_WIDTH + 2 * POOL_WIDTH + N_BRANCH * D_MODEL

kernel_name = 'hybrid_dsa_rglru_pool_stream_step'


def rmsnorm(x, g):
    xf = x.astype(jnp.float32)
    y = xf * lax.rsqrt(jnp.mean(xf * xf, axis=-1, keepdims=True) + RMS_EPS) * g.astype(jnp.float32)
    return y.astype(x.dtype)


def rope(x, pos):
    half = ROT_DIM // 2
    inv = ROPE_THETA ** (-jnp.arange(0, ROT_DIM, 2, dtype=jnp.float32) / ROT_DIM)
    ang = pos.astype(jnp.float32)[:, None] * inv[None, :]
    cos = jnp.cos(ang)[None, :, None, :]
    sin = jnp.sin(ang)[None, :, None, :]
    xf = x.astype(jnp.float32)
    x1 = xf[..., :half]
    x2 = xf[..., half:ROT_DIM]
    out = jnp.concatenate([x1 * cos - x2 * sin, x2 * cos + x1 * sin, xf[..., ROT_DIM:]], axis=-1)
    return out.astype(x.dtype)


def sparse_attention(q, k, v, qi, ki, wi, q_chunk, k_chunk, topk):
    B, T = q.shape[0], q.shape[1]
    blk = min(QBLOCK, T)
    nb = -(-T // blk)
    pad = nb * blk - T

    def to_blocks(a):
        a = jnp.pad(a, [(0, 0), (0, pad)] + [(0, 0)] * (a.ndim - 2))
        return jnp.moveaxis(a.reshape((B, nb, blk) + a.shape[2:]), 1, 0)

    qc = jnp.pad(q_chunk, (0, pad), constant_values=PAD_CHUNK).reshape(nb, blk)
    scale = HEAD_DIM ** -0.5

    def block(args):
        qb, qib, wib, qcb = args
        s_idx = jax.nn.relu(jnp.einsum('bthd,bsd->bths', qib, ki))
        score = jnp.einsum('bth,bths->bts', wib, s_idx).astype(jnp.float32)
        ok = k_chunk[None, :] <= qcb[:, None]
        score = jnp.where(ok[None], score, -jnp.inf)
        top_val, top_idx = lax.top_k(score, topk)
        valid = jnp.isfinite(top_val)
        kg = jax.vmap(lambda kk, ii: kk[ii])(k, top_idx)
        vg = jax.vmap(lambda vv, ii: vv[ii])(v, top_idx)
        logits = jnp.einsum('bthd,btjhd->bthj', qb, kg).astype(jnp.float32) * scale
        logits = jnp.where(valid[:, :, None, :], logits, -jnp.inf)
        p = jax.nn.softmax(logits, axis=-1).astype(v.dtype)
        return jnp.einsum('bthj,btjhd->bthd', p, vg)

    out = lax.map(block, (to_blocks(q), to_blocks(qi), to_blocks(wi), qc))
    out = jnp.moveaxis(out, 0, 1).reshape((B, nb * blk) + out.shape[3:])
    return out[:, :T]


def rglru(xb, conv_st, h0, conv_w, conv_b, wa, ba, wx, bx, lam):
    B, T, W = xb.shape
    xp = jnp.concatenate([conv_st.astype(xb.dtype), xb], axis=1)
    xc = conv_b + xp[:, 0:T] * conv_w[0]
    for j in range(1, CONV_WIDTH):
        xc = xc + xp[:, j:j + T] * conv_w[j]
    xblk = xc.reshape(B, T, LRU_BLOCKS, LRU_BLOCK_DIM)
    r = jax.nn.sigmoid(jnp.einsum('btnc,ncd->btnd', xblk, wa).reshape(B, T, W) + ba)
    i = jax.nn.sigmoid(jnp.einsum('btnc,ncd->btnd', xblk, wx).reshape(B, T, W) + bx)
    log_a = -LRU_C * r.astype(jnp.float32) * jax.nn.softplus(-lam.astype(jnp.float32))
    a = jnp.exp(log_a)
    b = jnp.sqrt(-jnp.expm1(2.0 * log_a)) * (i * xc).astype(jnp.float32)
    b = b.at[:, 0].add(a[:, 0] * h0.astype(jnp.float32))

    def combine(e1, e2):
        a1, b1 = e1
        a2, b2 = e2
        return a1 * a2, a2 * b1 + b2

    _, h = lax.associative_scan(combine, (a, b), axis=1)
    return h.astype(xb.dtype), xp[:, -(CONV_WIDTH - 1):], h[:, -1].astype(xb.dtype)


def pool_mix(xc, pool_st, n_hist, pool_w, pool_scale):
    B, T, W = xc.shape
    xp = jnp.concatenate([pool_st.astype(xc.dtype), xc], axis=1).astype(jnp.float32)
    cs = jnp.concatenate([jnp.zeros((B, 1, W), jnp.float32), jnp.cumsum(xp, axis=1)], axis=1)
    t = jnp.arange(T)
    outs = []
    for g, w in enumerate(POOL_WINDOWS):
        sl = slice(g * POOL_GROUP_DIM, (g + 1) * POOL_GROUP_DIM)
        upper = cs[:, POOL_HIST + 1:POOL_HIST + 1 + T, sl]
        lower = cs[:, POOL_HIST + 1 - w:POOL_HIST + 1 - w + T, sl]
        cnt = jnp.minimum(w, t + 1 + n_hist).astype(jnp.float32)[None, :, None]
        outs.append((upper - lower) / cnt)
    pooled = jnp.concatenate(outs, axis=-1) - xp[:, POOL_HIST:]
    mixed = jnp.einsum('btgc,gcd->btgd', pooled.reshape(B, T, POOL_GROUPS, POOL_GROUP_DIM),
                       pool_w.astype(jnp.float32)).reshape(B, T, W) * pool_scale.astype(jnp.float32)
    return mixed.astype(xc.dtype), xp[:, -POOL_HIST:].astype(xc.dtype)


def mixer_layer(x, pos, q_chunk, k_chunk, topk, n_hist, k_past, v_past, ki_past, conv_st, lru_st, pool_st,
                norm_g, w_in, conv_w, conv_b, lru_wa, lru_ba, lru_wx, lru_bx, lru_lambda, pool_w, pool_scale,
                w_branch_out, w_out):
    B, T, _ = x.shape
    hn = rmsnorm(x, norm_g)
    proj = jnp.einsum('btd,dn->btn', hn, w_in)
    splits = np.cumsum(COL_SIZES)[:-1].tolist()
    q, k, v, ga, qi, ki, wi, xb, gb, xc, gc, gm = jnp.split(proj, splits, axis=-1)
    q = rope(q.reshape(B, T, N_HEADS, HEAD_DIM), pos)
    k = rope(k.reshape(B, T, N_HEADS, HEAD_DIM), pos)
    v = v.reshape(B, T, N_HEADS, HEAD_DIM)
    qi = rope(qi.reshape(B, T, IDX_HEADS, IDX_DIM), pos)
    ki = rope(ki[:, :, None, :], pos)[:, :, 0]
    k_all = jnp.concatenate([k_past.astype(k.dtype), k], axis=1)
    v_all = jnp.concatenate([v_past.astype(v.dtype), v], axis=1)
    ki_all = jnp.concatenate([ki_past.astype(ki.dtype), ki], axis=1)
    attn = sparse_attention(q, k_all, v_all, qi, ki_all, wi, q_chunk, k_chunk, topk).reshape(B, T, ATTN_WIDTH)
    y_a = attn * jax.nn.silu(ga)
    hb, conv_new, lru_new = rglru(xb, conv_st, lru_st, conv_w, conv_b, lru_wa, lru_ba, lru_wx, lru_bx, lru_lambda)
    y_b = hb * jax.nn.silu(gb)
    hc, pool_new = pool_mix(xc, pool_st, n_hist, pool_w, pool_scale)
    y_c = hc * jax.nn.silu(gc)
    branches = jnp.stack([y_a, y_b, y_c], axis=2)
    proj_b = jnp.einsum('btnc,ncd->btnd', branches, w_branch_out)
    gates = jax.nn.sigmoid(gm.reshape(B, T, N_BRANCH, D_MODEL))
    merged = jnp.sum(gates * proj_b, axis=2)
    out = x + jnp.einsum('btd,de->bte', merged, w_out)
    return out, (k, v, ki, conv_new, lru_new, pool_new)


def run_trunk(x, pos, q_chunk, k_chunk, topk, n_hist, k_past, v_past, ki_past, conv_st, lru_st, pool_st,
              norm_g, w_in, conv_w, conv_b, lru_wa, lru_ba, lru_wx, lru_bx, lru_lambda, pool_w, pool_scale,
              w_branch_out, w_out, final_norm_g):
    news = [[], [], [], [], [], []]
    for l in range(DEPTH):
        x, new = mixer_layer(x, pos, q_chunk, k_chunk, topk, n_hist, k_past[l], v_past[l], ki_past[l],
                             conv_st[l], lru_st[l], pool_st[l], norm_g[l], w_in[l], conv_w[l], conv_b[l],
                             lru_wa[l], lru_ba[l], lru_wx[l], lru_bx[l], lru_lambda[l], pool_w[l],
                             pool_scale[l], w_branch_out[l], w_out[l])
        for lst, arr in zip(news, new):
            lst.append(arr)
    y = rmsnorm(x, final_norm_g)
    k_n, v_n, ki_n, conv_n, lru_n, pool_n = [jnp.stack(lst) for lst in news]
    return y, k_n, v_n, ki_n, conv_n, lru_n, pool_n


def setup_inputs(seed: int = 0) -> dict:
    key = jax.random.key(seed)
    ks = jax.random.split(key, 24)
    f32 = jnp.float32
    nrm = lambda k, shape, s: jax.random.normal(k, shape, f32) * s
    u = jax.random.uniform(ks[16], (DEPTH, LRU_WIDTH), f32, minval=0.9, maxval=0.999)
    a0 = u ** (1.0 / LRU_C)
    lru_lambda = jnp.log(a0) - jnp.log1p(-a0)
    return {
        'x_prompt': nrm(ks[0], (BATCH, SEQ, D_MODEL), 1.0),
        'x_sample': nrm(ks[1], (DEC_BATCH, DEC_SEQ, D_MODEL), 1.0),
        'cache_k': nrm(ks[2], (DEPTH, DEC_BATCH, PAST_LEN, N_HEADS, HEAD_DIM), 1.0),
        'cache_v': nrm(ks[3], (DEPTH, DEC_BATCH, PAST_LEN, N_HEADS, HEAD_DIM), 1.0),
        'cache_kidx': nrm(ks[4], (DEPTH, DEC_BATCH, PAST_LEN, IDX_DIM), 1.0),
        'state_conv': nrm(ks[5], (DEPTH, DEC_BATCH, CONV_WIDTH - 1, LRU_WIDTH), 1.0),
        'state_lru': nrm(ks[6], (DEPTH, DEC_BATCH, LRU_WIDTH), 0.5),
        'state_pool': nrm(ks[7], (DEPTH, DEC_BATCH, POOL_HIST, POOL_WIDTH), 1.0),
        'meta_tokens': nrm(ks[8], (N_META, D_MODEL), 1.0),
        'norm_g': 1.0 + nrm(ks[9], (DEPTH, D_MODEL), 0.02),
        'w_in': nrm(ks[10], (DEPTH, D_MODEL, N_IN), D_MODEL ** -0.5),
        'conv_w': nrm(ks[11], (DEPTH, CONV_WIDTH, LRU_WIDTH), CONV_WIDTH ** -0.5),
        'conv_b': nrm(ks[12], (DEPTH, LRU_WIDTH), 0.01),
        'lru_wa': nrm(ks[13], (DEPTH, LRU_BLOCKS, LRU_BLOCK_DIM, LRU_BLOCK_DIM), LRU_BLOCK_DIM ** -0.5),
        'lru_ba': nrm(ks[14], (DEPTH, LRU_WIDTH), 0.01),
        'lru_wx': nrm(ks[15], (DEPTH, LRU_BLOCKS, LRU_BLOCK_DIM, LRU_BLOCK_DIM), LRU_BLOCK_DIM ** -0.5),
        'lru_bx': nrm(ks[17], (DEPTH, LRU_WIDTH), 0.01),
        'lru_lambda': lru_lambda,
        'pool_w': nrm(ks[18], (DEPTH, POOL_GROUPS, POOL_GROUP_DIM, POOL_GROUP_DIM), POOL_GROUP_DIM ** -0.5),
        'pool_scale': 1.0 + nrm(ks[19], (DEPTH, POOL_WIDTH), 0.02),
        'w_branch_out': nrm(ks[20], (DEPTH, N_BRANCH, BRANCH_WIDTH, D_MODEL), BRANCH_WIDTH ** -0.5),
        'w_out': nrm(ks[21], (DEPTH, D_MODEL, D_MODEL), D_MODEL ** -0.5),
        'final_norm_g': 1.0 + nrm(ks[22], (D_MODEL,), 0.02),
    }


def reference(x_prompt, x_sample, cache_k, cache_v, cache_kidx, state_conv, state_lru, state_pool,
              meta_tokens, norm_g, w_in, conv_w, conv_b, lru_wa, lru_ba, lru_wx, lru_bx, lru_lambda,
              pool_w, pool_scale, w_branch_out, w_out, final_norm_g):
    weights = (norm_g, w_in, conv_w, conv_b, lru_wa, lru_ba, lru_wx, lru_bx, lru_lambda, pool_w, pool_scale,
               w_branch_out, w_out, final_norm_g)
    dt = x_prompt.dtype
    B, S, _ = x_prompt.shape
    meta = jnp.broadcast_to(meta_tokens.astype(dt)[None], (B, N_META, D_MODEL))
    x0 = jnp.concatenate([meta, x_prompt], axis=1)
    pos_p = jnp.arange(N_META + S, dtype=jnp.int32)
    chunk_p = jnp.concatenate([jnp.zeros((N_META,), jnp.int32), jnp.arange(S, dtype=jnp.int32) // CHUNK + 1])
    y_full, k_p, v_p, ki_p, conv_p, lru_p, pool_p = run_trunk(
        x0, pos_p, chunk_p, chunk_p, min(TOPK_MAX, S // 4), 0,
        jnp.zeros((DEPTH, B, 0, N_HEADS, HEAD_DIM), dt), jnp.zeros((DEPTH, B, 0, N_HEADS, HEAD_DIM), dt),
        jnp.zeros((DEPTH, B, 0, IDX_DIM), dt), jnp.zeros((DEPTH, B, CONV_WIDTH - 1, LRU_WIDTH), dt),
        jnp.zeros((DEPTH, B, LRU_WIDTH), dt), jnp.zeros((DEPTH, B, POOL_HIST, POOL_WIDTH), dt),
        *weights)
    y_prompt = y_full[:, N_META:]
    T1 = x_sample.shape[1]
    P = cache_k.shape[2]
    pos_s = P + jnp.arange(T1, dtype=jnp.int32)
    y_sample, k_s, v_s, ki_s, conv_s, lru_s, pool_s = run_trunk(
        x_sample, pos_s, jnp.zeros((T1,), jnp.int32), jnp.zeros((P + T1,), jnp.int32),
        min(TOPK_MAX, (P + T1) // 4), P,
        cache_k, cache_v, cache_kidx, state_conv, state_lru, state_pool, *weights)
    return (y_prompt, y_sample, k_p, v_p, ki_p, conv_p, lru_p, pool_p, k_s, v_s, ki_s, conv_s, lru_s, pool_s)
```

```python
import functools

import numpy as np
import jax
import jax.numpy as jnp
from jax import lax
from jax.experimental import pallas as pl
from jax.experimental.pallas import tpu as pltpu

F32 = jnp.float32
BF16 = jnp.bfloat16

D_MODEL = 1024
CHUNK = 64
N_META = 16
RMS_EPS = 1e-6
N_HEADS = 8
HEAD_DIM = 64
ATTN_WIDTH = N_HEADS * HEAD_DIM
ROT_DIM = HEAD_DIM // 4
ROPE_THETA = 500000.0
IDX_HEADS = 4
IDX_DIM = 64
TOPK_MAX = 256
LRU_WIDTH = 512
LRU_BLOCKS = 8
LRU_BLOCK_DIM = LRU_WIDTH // LRU_BLOCKS
CONV_WIDTH = 4
LRU_C = 8.0
POOL_WIDTH = 512
POOL_WINDOWS = (2, 4, 8, 16)
POOL_GROUPS = 4
POOL_GROUP_DIM = POOL_WIDTH // POOL_GROUPS
POOL_HIST = 15
N_BRANCH = 3
BRANCH_WIDTH = 512
COL_SIZES = (ATTN_WIDTH, ATTN_WIDTH, ATTN_WIDTH, ATTN_WIDTH, IDX_HEADS * IDX_DIM, IDX_DIM, IDX_HEADS,
             LRU_WIDTH, LRU_WIDTH, POOL_WIDTH, POOL_WIDTH, N_BRANCH * D_MODEL)

LANES = 128
SUBLANES = 8

COL_Q = 0
COL_K = 512
COL_V = 1024
COL_GA = 1536
COL_XB = 2048
COL_GB = 2560
COL_XC = 3072
COL_GC = 3584
COL_GM = 4096
COL_QI = 7168
COL_KIW = 7424
N_PACKED = 7552
PROJ_CHUNK = 512

ROW_TILE = 256
KEY_TILE = 256
SEL_ROWS = 64
NEG_INF = float("-inf")
POS_INF = float("inf")
MASK_LOGIT = -1e30
VMEM_LIMIT = 56 * 1024 * 1024


def _cparams(sem):
    return pltpu.CompilerParams(dimension_semantics=sem, vmem_limit_bytes=VMEM_LIMIT)


def _rope_slab(y, c, s1, s2):
    half = ROT_DIM // 2
    return y * c + pltpu.roll(y, LANES - half, 1) * s1 + pltpu.roll(y, half, 1) * s2


def _proj_kernel(x_ref, g_ref, w_ref, rope_ref, main_ref, kb_ref, vb_ref):
    x = x_ref[...]
    ms = jnp.mean(x * x, axis=-1, keepdims=True)
    hn = (x * lax.rsqrt(ms + RMS_EPS) * g_ref[...]).astype(BF16)
    c = rope_ref[:, 0:LANES]
    s1 = rope_ref[:, LANES:2 * LANES]
    s2 = rope_ref[:, 2 * LANES:3 * LANES]
    lane = lax.broadcasted_iota(jnp.int32, (x.shape[0], LANES), 1)
    for c0 in range(0, N_PACKED, PROJ_CHUNK):
        cw = min(PROJ_CHUNK, N_PACKED - c0)
        y = jnp.dot(hn, w_ref[:, c0:c0 + cw], preferred_element_type=F32)
        if c0 in (COL_Q, COL_K, COL_QI):
            slabs = []
            for j in range(cw // LANES):
                slab = y[:, j * LANES:(j + 1) * LANES]
                roped = _rope_slab(slab, c, s1, s2)
                if c0 + j * LANES == COL_KIW:
                    roped = jnp.where(lane < IDX_DIM, roped, slab)
                slabs.append(roped)
            y = jnp.concatenate(slabs, axis=1)
        main_ref[:, c0:c0 + cw] = y
        if c0 == COL_K:
            kb_ref[...] = y.astype(BF16)
        if c0 == COL_V:
            vb_ref[...] = y.astype(BF16)


def _proj(x2d, g, wp, rope, rope_tiles):
    m = x2d.shape[0]
    tm = ROW_TILE
    return pl.pallas_call(
        _proj_kernel,
        grid=(m // tm,),
        in_specs=[
            pl.BlockSpec((tm, D_MODEL), lambda i: (i, 0)),
            pl.BlockSpec((1, D_MODEL), lambda i: (0, 0)),
            pl.BlockSpec((D_MODEL, N_PACKED), lambda i: (0, 0), pipeline_mode=pl.Buffered(1)),
            pl.BlockSpec((tm, 3 * LANES), lambda i: (i % rope_tiles, 0)),
        ],
        out_specs=[
            pl.BlockSpec((tm, N_PACKED), lambda i: (i, 0)),
            pl.BlockSpec((tm, ATTN_WIDTH), lambda i: (i, 0)),
            pl.BlockSpec((tm, ATTN_WIDTH), lambda i: (i, 0)),
        ],
        out_shape=[
            jax.ShapeDtypeStruct((m, N_PACKED), F32),
            jax.ShapeDtypeStruct((m, ATTN_WIDTH), BF16),
            jax.ShapeDtypeStruct((m, ATTN_WIDTH), BF16),
        ],
        compiler_params=_cparams(("arbitrary",)),
        name="proj",
    )(x2d, g, wp, rope)


def _attn_kernel(q_ref, qi_ref, wq_ref, kb_ref, vb_ref, kiw_ref, tri_ref, o_ref,
                 sc_ref, m_ref, l_ref, acc_ref, *, tq, topk, causal, frames_start, k_lo, k_hi, n_kt_total):
    i = pl.program_id(1)
    tk = KEY_TILE
    kf = float(topk)
    n_kt = jnp.minimum(i + 1, n_kt_total) if causal else n_kt_total

    qrow = i * tq + lax.broadcasted_iota(jnp.int32, (tq, 1), 0)
    if causal:
        qchunk = jnp.where(qrow < frames_start, 0, (qrow - frames_start) // CHUNK + 1)

    qi_b = qi_ref[...].astype(BF16)
    w = wq_ref[:, IDX_DIM:IDX_DIM + IDX_HEADS]

    def score_tile(kt, carry):
        n_adm, mn, mx = carry
        k0 = pl.multiple_of(kt * tk, tk)
        ki_t = kiw_ref[pl.ds(k0, tk), 0:IDX_DIM].astype(BF16)
        s = jnp.zeros((tq, tk), F32)
        for h in range(IDX_HEADS):
            d = lax.dot_general(qi_b[:, h * IDX_DIM:(h + 1) * IDX_DIM], ki_t,
                                (((1,), (1,)), ((), ())), preferred_element_type=F32)
            s = s + w[:, h:h + 1] * jnp.maximum(d, 0.0)
        krow = k0 + lax.broadcasted_iota(jnp.int32, (1, tk), 1)
        adm = (krow >= k_lo) & (krow < k_hi)
        if causal:
            kchunk = jnp.where(krow < frames_start, 0, (krow - frames_start) // CHUNK + 1)
            adm = adm & (kchunk <= qchunk)
        s = jnp.where(adm, s, NEG_INF)
        sc_ref[:, pl.ds(k0, tk)] = s
        n_adm = n_adm + jnp.sum(jnp.where(adm, 1.0, 0.0), axis=1, keepdims=True)
        mn = jnp.minimum(mn, jnp.min(jnp.where(adm, s, POS_INF), axis=1, keepdims=True))
        mx = jnp.maximum(mx, jnp.max(s, axis=1, keepdims=True))
        return n_adm, mn, mx

    n_adm, mn, mx = lax.fori_loop(
        0, n_kt, score_tile,
        (jnp.zeros((tq, 1), F32), jnp.full((tq, 1), POS_INF, F32), jnp.full((tq, 1), NEG_INF, F32)))

    few = n_adm <= kf
    lo0 = jnp.where(few, NEG_INF, mn)
    hi0 = jnp.where(few, NEG_INF, mx)
    n_kc = n_kt * (tk // LANES)

    def row_pass(rb, thr):
        thr_b = jnp.broadcast_to(thr, (SEL_ROWS, LANES))

        def body(kc, c):
            cnt, dn, up = c
            s = sc_ref[rb * SEL_ROWS:(rb + 1) * SEL_ROWS, pl.ds(pl.multiple_of(kc * LANES, LANES), LANES)]
            gt = s > thr_b
            cnt = cnt + jnp.where(gt, 1.0, 0.0)
            dn = jnp.maximum(dn, jnp.where(gt, NEG_INF, s))
            up = jnp.minimum(up, jnp.where(gt, s, POS_INF))
            return cnt, dn, up

        cnt, dn, up = lax.fori_loop(
            0, n_kc, body,
            (jnp.zeros((SEL_ROWS, LANES), F32), jnp.full((SEL_ROWS, LANES), NEG_INF, F32),
             jnp.full((SEL_ROWS, LANES), POS_INF, F32)))
        return (jnp.sum(cnt, axis=1, keepdims=True), jnp.max(dn, axis=1, keepdims=True),
                jnp.min(up, axis=1, keepdims=True))

    def search_cond(c):
        lo, hi = c
        return jnp.sum(jnp.where(lo < hi, 1.0, 0.0)) > 0.0

    def search_body(c):
        lo, hi = c
        new_lo, new_hi = [], []
        for rb in range(tq // SEL_ROWS):
            lo_b = lo[rb * SEL_ROWS:(rb + 1) * SEL_ROWS]
            hi_b = hi[rb * SEL_ROWS:(rb + 1) * SEL_ROWS]
            mid = 0.5 * lo_b + 0.5 * hi_b
            mid = jnp.where(mid >= hi_b, lo_b, mid)
            cnt, dn, up = row_pass(rb, mid)
            active = lo_b < hi_b
            ge = cnt >= kf
            new_lo.append(jnp.where(active & ge, up, lo_b))
            new_hi.append(jnp.where(active & jnp.logical_not(ge), dn, hi_b))
        return jnp.concatenate(new_lo, axis=0), jnp.concatenate(new_hi, axis=0)

    thr, _ = lax.while_loop(search_cond, search_body, (lo0, hi0))
    cnt_gt = jnp.concatenate(
        [row_pass(rb, thr[rb * SEL_ROWS:(rb + 1) * SEL_ROWS])[0] for rb in range(tq // SEL_ROWS)], axis=0)
    need = kf - cnt_gt

    q_b = (q_ref[...] * (HEAD_DIM ** -0.5)).astype(BF16)
    m_ref[...] = jnp.full(m_ref.shape, MASK_LOGIT, F32)
    l_ref[...] = jnp.zeros(l_ref.shape, F32)
    acc_ref[...] = jnp.zeros(acc_ref.shape, F32)

    def attend_tile(kt, run):
        k0 = pl.multiple_of(kt * tk, tk)
        s = sc_ref[:, pl.ds(k0, tk)]
        eq = (s == thr) & (s > NEG_INF)
        eqf = jnp.where(eq, 1.0, 0.0)
        rank = run + jnp.dot(eqf.astype(BF16), tri_ref[...], preferred_element_type=F32)
        sel = (s > thr) | (eq & (rank < need))
        run = run + jnp.sum(eqf, axis=1, keepdims=True)
        for h in range(N_HEADS):
            hs = slice(h * HEAD_DIM, (h + 1) * HEAD_DIM)
            lg = lax.dot_general(q_b[:, hs], kb_ref[pl.ds(k0, tk), hs],
                                 (((1,), (1,)), ((), ())), preferred_element_type=F32)
            m_old = m_ref[h][:, 0:1]
            m_new = jnp.maximum(m_old, jnp.max(jnp.where(sel, lg, MASK_LOGIT), axis=1, keepdims=True))
            p = jnp.where(sel, jnp.exp(lg - m_new), 0.0)
            alpha = jnp.exp(m_old - m_new)
            l_new = alpha * l_ref[h][:, 0:1] + jnp.sum(p, axis=1, keepdims=True)
            pv = jnp.dot(p.astype(BF16), vb_ref[pl.ds(k0, tk), hs], preferred_element_type=F32)
            acc_ref[h] = alpha * acc_ref[h] + pv
            m_ref[h] = jnp.broadcast_to(m_new, (tq, LANES))
            l_ref[h] = jnp.broadcast_to(l_new, (tq, LANES))
        return run

    lax.fori_loop(0, n_kt, attend_tile, jnp.zeros((tq, 1), F32))
    o_ref[...] = jnp.concatenate([acc_ref[h] / l_ref[h][:, 0:1] for h in range(N_HEADS)], axis=1)


def _attention(main, kb, vb, kiw, kiw_col, tri, *, batch, tq, topk, causal, frames_start, k_lo, k_hi):
    n_q = main.shape[0] // batch // tq
    s_len = kb.shape[0] // batch
    n_kt_total = s_len // KEY_TILE
    kern = functools.partial(_attn_kernel, tq=tq, topk=topk, causal=causal, frames_start=frames_start,
                             k_lo=k_lo, k_hi=k_hi, n_kt_total=n_kt_total)
    return pl.pallas_call(
        kern,
        grid=(batch, n_q),
        in_specs=[
            pl.BlockSpec((tq, ATTN_WIDTH), lambda b, i: (b * n_q + i, COL_Q // ATTN_WIDTH)),
            pl.BlockSpec((tq, IDX_HEADS * IDX_DIM), lambda b, i: (b * n_q + i, COL_QI // (IDX_HEADS * IDX_DIM))),
            pl.BlockSpec((tq, LANES), lambda b, i: (b * n_q + i, COL_KIW // LANES)),
            pl.BlockSpec((s_len, ATTN_WIDTH), lambda b, i: (b, 0)),
            pl.BlockSpec((s_len, ATTN_WIDTH), lambda b, i: (b, 0)),
            pl.BlockSpec((s_len, LANES), lambda b, i: (b, kiw_col)),
            pl.BlockSpec((KEY_TILE, KEY_TILE), lambda b, i: (0, 0)),
        ],
        out_specs=pl.BlockSpec((tq, ATTN_WIDTH), lambda b, i: (b * n_q + i, 0)),
        out_shape=jax.ShapeDtypeStruct((main.shape[0], ATTN_WIDTH), F32),
        scratch_shapes=[
            pltpu.VMEM((tq, s_len), F32),
            pltpu.VMEM((N_HEADS, tq, LANES), F32),
            pltpu.VMEM((N_HEADS, tq, LANES), F32),
            pltpu.VMEM((N_HEADS, tq, HEAD_DIM), F32),
        ],
        compiler_params=_cparams(("arbitrary", "arbitrary")),
        name="attn",
    )(main, main, main, kb, vb, kiw, tri)


def _lru_kernel(xb_ref, cst_ref, h0_ref, cw_ref, cb_ref, wa_ref, ba_ref, wx_ref, bx_ref, lam_ref,
                hb_ref, hlast_ref, xpad_ref, a_ref, b_ref, h_ref, *, tt, n_pad):
    t = pl.program_id(1)
    hist = SUBLANES

    @pl.when(t == 0)
    def _():
        xpad_ref[0:hist, :] = jnp.zeros((hist, LRU_WIDTH), F32)
        xpad_ref[hist - (CONV_WIDTH - 1):hist, :] = cst_ref[0]
        h_ref[...] = h0_ref[0]

    xpad_ref[hist:hist + tt, :] = xb_ref[...]
    xc = jnp.broadcast_to(cb_ref[...], (tt, LRU_WIDTH))
    for j in range(CONV_WIDTH):
        off = hist - (CONV_WIDTH - 1) + j
        xc = xc + xpad_ref[off:off + tt, :] * cw_ref[j:j + 1, :]
    xpad_ref[0:hist, :] = xpad_ref[tt:tt + hist, :]

    xcb = xc.astype(BF16)
    r = jax.nn.sigmoid(jnp.dot(xcb, wa_ref[...], preferred_element_type=F32) + ba_ref[...])
    g = jax.nn.sigmoid(jnp.dot(xcb, wx_ref[...], preferred_element_type=F32) + bx_ref[...])
    log_a = -LRU_C * r * jax.nn.softplus(-lam_ref[...])
    a = jnp.exp(log_a)
    b = jnp.sqrt(1.0 - a * a) * (g * xc)
    if n_pad:
        row = t * tt + lax.broadcasted_iota(jnp.int32, (tt, 1), 0)
        b = jnp.where(row >= n_pad, b, 0.0)
    a_ref[...] = a
    b_ref[...] = b

    def group(gi, h):
        r0 = pl.multiple_of(gi * SUBLANES, SUBLANES)
        ag = a_ref[pl.ds(r0, SUBLANES), :]
        bg = b_ref[pl.ds(r0, SUBLANES), :]
        rows = []
        for j in range(SUBLANES):
            h = ag[j:j + 1, :] * h + bg[j:j + 1, :]
            rows.append(h)
        hb_ref[pl.ds(r0, SUBLANES), :] = jnp.concatenate(rows, axis=0)
        return h

    h = lax.fori_loop(0, tt // SUBLANES, group, h_ref[...])
    h_ref[...] = h
    hlast_ref[0] = h


def _lru(main, conv_st, h0, lw, *, batch, tt, n_pad):
    n_t = main.shape[0] // batch // tt
    kern = functools.partial(_lru_kernel, tt=tt, n_pad=n_pad)
    const = lambda shape: pl.BlockSpec(shape, lambda b, t: (0,) * len(shape))
    return pl.pallas_call(
        kern,
        grid=(batch, n_t),
        in_specs=[
            pl.BlockSpec((tt, LRU_WIDTH), lambda b, t: (b * n_t + t, COL_XB // LRU_WIDTH)),
            pl.BlockSpec((1, CONV_WIDTH - 1, LRU_WIDTH), lambda b, t: (b, 0, 0)),
            pl.BlockSpec((1, 1, LRU_WIDTH), lambda b, t: (b, 0, 0)),
            const((CONV_WIDTH, LRU_WIDTH)), const((1, LRU_WIDTH)),
            const((LRU_WIDTH, LRU_WIDTH)), const((1, LRU_WIDTH)),
            const((LRU_WIDTH, LRU_WIDTH)), const((1, LRU_WIDTH)),
            const((1, LRU_WIDTH)),
        ],
        out_specs=[
            pl.BlockSpec((tt, LRU_WIDTH), lambda b, t: (b * n_t + t, 0)),
            pl.BlockSpec((1, 1, LRU_WIDTH), lambda b, t: (b, 0, 0)),
        ],
        out_shape=[
            jax.ShapeDtypeStruct((main.shape[0], LRU_WIDTH), F32),
            jax.ShapeDtypeStruct((batch, 1, LRU_WIDTH), F32),
        ],
        scratch_shapes=[
            pltpu.VMEM((tt + SUBLANES, LRU_WIDTH), F32),
            pltpu.VMEM((tt, LRU_WIDTH), F32),
            pltpu.VMEM((tt, LRU_WIDTH), F32),
            pltpu.VMEM((1, LRU_WIDTH), F32),
        ],
        compiler_params=_cparams(("arbitrary", "arbitrary")),
        name="lru",
    )(main, conv_st, h0, lw["conv_w"], lw["conv_b"], lw["wa"], lw["ba"], lw["wx"], lw["bx"], lw["lam"])


def _pool_kernel(xc_ref, pst_ref, pw_ref, ps_ref, hc_ref, xpad_ref, *, tt, n_pad, n_hist):
    t = pl.program_id(1)
    hist = 2 * SUBLANES

    @pl.when(t == 0)
    def _():
        xpad_ref[0:hist, :] = jnp.zeros((hist, POOL_WIDTH), F32)
        xpad_ref[hist - POOL_HIST:hist, :] = pst_ref[0]

    x = xc_ref[...]
    xpad_ref[hist:hist + tt, :] = x
    row = t * tt + lax.broadcasted_iota(jnp.int32, (tt, 1), 0)
    seen = jnp.maximum(row - n_pad + 1 + n_hist, 1).astype(F32)
    outs = []
    for gi, win in enumerate(POOL_WINDOWS):
        cs = slice(gi * POOL_GROUP_DIM, (gi + 1) * POOL_GROUP_DIM)
        tot = x[:, cs]
        for j in range(1, win):
            tot = tot + xpad_ref[hist - j:hist - j + tt, cs]
        pooled = tot / jnp.minimum(float(win), seen) - x[:, cs]
        outs.append(jnp.dot(pooled.astype(BF16), pw_ref[gi], preferred_element_type=F32))
    hc_ref[...] = jnp.concatenate(outs, axis=1) * ps_ref[...]
    xpad_ref[0:hist, :] = xpad_ref[tt:tt + hist, :]


def _pool(main, pool_st, pw, ps, *, batch, tt, n_pad, n_hist):
    n_t = main.shape[0] // batch // tt
    kern = functools.partial(_pool_kernel, tt=tt, n_pad=n_pad, n_hist=n_hist)
    return pl.pallas_call(
        kern,
        grid=(batch, n_t),
        in_specs=[
            pl.BlockSpec((tt, POOL_WIDTH), lambda b, t: (b * n_t + t, COL_XC // POOL_WIDTH)),
            pl.BlockSpec((1, POOL_HIST, POOL_WIDTH), lambda b, t: (b, 0, 0)),
            pl.BlockSpec((POOL_GROUPS, POOL_GROUP_DIM, POOL_GROUP_DIM), lambda b, t: (0, 0, 0)),
            pl.BlockSpec((1, POOL_WIDTH), lambda b, t: (0, 0)),
        ],
        out_specs=pl.BlockSpec((tt, POOL_WIDTH), lambda b, t: (b * n_t + t, 0)),
        out_shape=jax.ShapeDtypeStruct((main.shape[0], POOL_WIDTH), F32),
        scratch_shapes=[pltpu.VMEM((tt + 2 * SUBLANES, POOL_WIDTH), F32)],
        compiler_params=_cparams(("arbitrary", "arbitrary")),
        name="pool",
    )(main, pool_st, pw, ps)


def _silu(v):
    return v * jax.nn.sigmoid(v)


def _merge_kernel(x_ref, at_ref, ga_ref, hb_ref, gb_ref, hc_ref, gc_ref, gm0_ref, gm1_ref, gm2_ref,
                  wb_ref, wo_ref, fg_ref, o_ref, *, tm, rows_per_seq, n_pad, final):
    ys = (at_ref[...] * _silu(ga_ref[...]), hb_ref[...] * _silu(gb_ref[...]), hc_ref[...] * _silu(gc_ref[...]))
    merged = jnp.zeros((tm, D_MODEL), F32)
    for n, (y, gm_ref) in enumerate(zip(ys, (gm0_ref, gm1_ref, gm2_ref))):
        pb = jnp.dot(y.astype(BF16), wb_ref[n], preferred_element_type=F32)
        merged = merged + jax.nn.sigmoid(gm_ref[...]) * pb
    out = x_ref[...] + jnp.dot(merged.astype(BF16), wo_ref[...], preferred_element_type=F32)
    if n_pad:
        row = (pl.program_id(0) * tm) % rows_per_seq + lax.broadcasted_iota(jnp.int32, (tm, 1), 0)
        out = jnp.where(row >= n_pad, out, 0.0)
    if final:
        ms = jnp.mean(out * out, axis=-1, keepdims=True)
        out = out * lax.rsqrt(ms + RMS_EPS) * fg_ref[...]
    o_ref[...] = out


def _merge(x2d, attn, hb, hc, main, wb, wo, fg, *, rows_per_seq, n_pad, final):
    m = x2d.shape[0]
    tm = ROW_TILE
    kern = functools.partial(_merge_kernel, tm=tm, rows_per_seq=rows_per_seq, n_pad=n_pad, final=final)
    half = lambda col: pl.BlockSpec((tm, BRANCH_WIDTH), lambda i: (i, col // BRANCH_WIDTH))
    full = lambda col: pl.BlockSpec((tm, D_MODEL), lambda i: (i, col // D_MODEL))
    return pl.pallas_call(
        kern,
        grid=(m // tm,),
        in_specs=[
            full(0), half(0), half(COL_GA), half(0), half(COL_GB), half(0), half(COL_GC),
            full(COL_GM), full(COL_GM + D_MODEL), full(COL_GM + 2 * D_MODEL),
            pl.BlockSpec((N_BRANCH, BRANCH_WIDTH, D_MODEL), lambda i: (0, 0, 0)),
            pl.BlockSpec((D_MODEL, D_MODEL), lambda i: (0, 0)),
            pl.BlockSpec((1, D_MODEL), lambda i: (0, 0)),
        ],
        out_specs=pl.BlockSpec((tm, D_MODEL), lambda i: (i, 0)),
        out_shape=jax.ShapeDtypeStruct((m, D_MODEL), F32),
        compiler_params=_cparams(("arbitrary",)),
        name="merge",
    )(x2d, attn, main, hb, main, hc, main, main, main, main, wb, wo, fg)


def _pack_w_in(w):
    offs = np.cumsum((0,) + COL_SIZES)
    q, k, v, ga, qi, ki, wi, xb, gb, xc, gc, gm = [w[:, offs[j]:offs[j + 1]] for j in range(len(COL_SIZES))]
    pad = jnp.zeros((w.shape[0], LANES - IDX_DIM - IDX_HEADS), w.dtype)
    return jnp.concatenate([q, k, v, ga, xb, gb, xc, gc, gm, qi, ki, wi, pad], axis=1).astype(BF16)


def _block_diag(w):
    nb, c, d = w.shape
    eye = jnp.eye(nb, dtype=w.dtype)
    return (w[:, :, None, :] * eye[:, None, :, None]).reshape(nb * c, nb * d)


def _rope_table(pos):
    half = ROT_DIM // 2
    inv = ROPE_THETA ** (-jnp.arange(0, ROT_DIM, 2, dtype=F32) / ROT_DIM)
    ang = pos.astype(F32)[:, None] * inv[None, :]
    cos, sin = jnp.cos(ang), jnp.sin(ang)
    n = pos.shape[0]
    rest1 = jnp.ones((n, HEAD_DIM - ROT_DIM), F32)
    rest0 = jnp.zeros((n, HEAD_DIM - ROT_DIM), F32)
    zh = jnp.zeros((n, half), F32)
    c = jnp.concatenate([cos, cos, rest1], axis=1)
    s1 = jnp.concatenate([-sin, zh, rest0], axis=1)
    s2 = jnp.concatenate([zh, sin, rest0], axis=1)
    return jnp.concatenate([c, c, s1, s1, s2, s2], axis=1)


def _layer_weights(l, norm_g, w_in, conv_w, conv_b, lru_wa, lru_ba, lru_wx, lru_bx, lru_lambda, pool_w,
                   pool_scale, w_branch_out, w_out):
    row = lambda v: v.reshape(1, -1)
    return dict(
        g=row(norm_g[l]), wp=_pack_w_in(w_in[l]),
        lru=dict(conv_w=conv_w[l], conv_b=row(conv_b[l]), wa=_block_diag(lru_wa[l]).astype(BF16), ba=row(lru_ba[l]),
                 wx=_block_diag(lru_wx[l]).astype(BF16), bx=row(lru_bx[l]), lam=row(lru_lambda[l])),
        pw=pool_w[l].astype(BF16), ps=row(pool_scale[l]),
        wb=w_branch_out[l].astype(BF16), wo=w_out[l].astype(BF16))


def _trunk(x2d, layers, final_g, rope, rope_tiles, tri, *, batch, rows, tq, n_pad, topk, causal, frames_start,
           n_hist, past, conv_st, lru_st, pool_st):
    depth = len(layers)
    news = []
    for l, lw in enumerate(layers):
        main, kb, vb = _proj(x2d, lw["g"], lw["wp"], rope, rope_tiles)
        if past is None:
            kb_all, vb_all, kiw_all, kiw_col = kb, vb, main, COL_KIW // LANES
            k_lo, k_hi = n_pad, rows
        else:
            pk, pv, pki = past[l]
            p_len = pk.shape[1]
            s_len = -(-(p_len + rows) // KEY_TILE) * KEY_TILE
            tail = s_len - p_len - rows

            def cat(p, new, width, dt):
                parts = [p.astype(dt), new.reshape(batch, rows, width).astype(dt)]
                if tail:
                    parts.append(jnp.zeros((batch, tail, width), dt))
                return jnp.concatenate(parts, axis=1).reshape(batch * s_len, width)

            kb_all = cat(pk, kb, ATTN_WIDTH, BF16)
            vb_all = cat(pv, vb, ATTN_WIDTH, BF16)
            pki_w = jnp.pad(pki, ((0, 0), (0, 0), (0, LANES - IDX_DIM)))
            kiw_all, kiw_col = cat(pki_w, main[:, COL_KIW:COL_KIW + LANES], LANES, F32), 0
            k_lo, k_hi = 0, p_len + rows
        attn = _attention(main, kb_all, vb_all, kiw_all, kiw_col, tri, batch=batch, tq=tq, topk=topk, causal=causal,
                          frames_start=frames_start, k_lo=k_lo, k_hi=k_hi)
        hb, h_last = _lru(main, conv_st[l], lru_st[l].reshape(batch, 1, LRU_WIDTH), lw["lru"],
                          batch=batch, tt=tq, n_pad=n_pad)
        hc = _pool(main, pool_st[l], lw["pw"], lw["ps"], batch=batch, tt=tq, n_pad=n_pad, n_hist=n_hist)
        x2d = _merge(x2d, attn, hb, hc, main, lw["wb"], lw["wo"], final_g, rows_per_seq=rows, n_pad=n_pad,
                     final=(l == depth - 1))
        m3 = main.reshape(batch, rows, N_PACKED)
        real = rows - n_pad
        k_new = m3[:, n_pad:, COL_K:COL_K + ATTN_WIDTH].reshape(batch, real, N_HEADS, HEAD_DIM)
        v_new = m3[:, n_pad:, COL_V:COL_V + ATTN_WIDTH].reshape(batch, real, N_HEADS, HEAD_DIM)
        ki_new = m3[:, n_pad:, COL_KIW:COL_KIW + IDX_DIM]
        xb_seq = jnp.concatenate([conv_st[l], m3[:, n_pad:, COL_XB:COL_XB + LRU_WIDTH][:, -(CONV_WIDTH - 1):]], axis=1)
        xc_seq = jnp.concatenate([pool_st[l], m3[:, n_pad:, COL_XC:COL_XC + POOL_WIDTH][:, -POOL_HIST:]], axis=1)
        news.append((k_new, v_new, ki_new, xb_seq[:, -(CONV_WIDTH - 1):], h_last.reshape(batch, LRU_WIDTH),
                     xc_seq[:, -POOL_HIST:]))
    stacked = [jnp.stack([n[j] for n in news]) for j in range(6)]
    return x2d, stacked


def kernel(x_prompt, x_sample, cache_k, cache_v, cache_kidx, state_conv, state_lru, state_pool, meta_tokens,
           norm_g, w_in, conv_w, conv_b, lru_wa, lru_ba, lru_wx, lru_bx, lru_lambda, pool_w, pool_scale,
           w_branch_out, w_out, final_norm_g):
    depth = w_in.shape[0]
    layers = [_layer_weights(l, norm_g, w_in, conv_w, conv_b, lru_wa, lru_ba, lru_wx, lru_bx, lru_lambda, pool_w,
                             pool_scale, w_branch_out, w_out) for l in range(depth)]
    final_g = final_norm_g.reshape(1, D_MODEL)
    tri = (lax.broadcasted_iota(jnp.int32, (KEY_TILE, KEY_TILE), 0)
           < lax.broadcasted_iota(jnp.int32, (KEY_TILE, KEY_TILE), 1)).astype(BF16)
    dt = x_prompt.dtype

    bp, seq, _ = x_prompt.shape
    assert seq % ROW_TILE == 0 and N_META <= ROW_TILE
    n_pad = ROW_TILE - N_META
    rows_p = ROW_TILE + seq
    x0 = jnp.concatenate([jnp.zeros((bp, n_pad, D_MODEL), dt),
                          jnp.broadcast_to(meta_tokens.astype(dt)[None], (bp, N_META, D_MODEL)), x_prompt], axis=1)
    pos_p = jnp.maximum(jnp.arange(rows_p, dtype=jnp.int32) - n_pad, 0)
    zeros = lambda *s: jnp.zeros((depth, bp) + s, dt)
    y_full, (k_p, v_p, ki_p, conv_p, lru_p, pool_p) = _trunk(
        x0.reshape(bp * rows_p, D_MODEL), layers, final_g, _rope_table(pos_p), rows_p // ROW_TILE, tri,
        batch=bp, rows=rows_p, tq=ROW_TILE, n_pad=n_pad, topk=min(TOPK_MAX, seq // 4), causal=True,
        frames_start=ROW_TILE, n_hist=0, past=None, conv_st=zeros(CONV_WIDTH - 1, LRU_WIDTH),
        lru_st=zeros(LRU_WIDTH), pool_st=zeros(POOL_HIST, POOL_WIDTH))
    y_prompt = y_full.reshape(bp, rows_p, D_MODEL)[:, ROW_TILE:]

    bs, t1, _ = x_sample.shape
    p_len = cache_k.shape[2]
    assert ROW_TILE % t1 == 0 and (bs * t1) % ROW_TILE == 0 and t1 % SEL_ROWS == 0
    pos_s = p_len + jnp.arange(t1, dtype=jnp.int32)
    rope_s = jnp.tile(_rope_table(pos_s), (ROW_TILE // t1, 1))
    past = [(cache_k[l].reshape(bs, p_len, ATTN_WIDTH), cache_v[l].reshape(bs, p_len, ATTN_WIDTH), cache_kidx[l])
            for l in range(depth)]
    y_s, (k_s, v_s, ki_s, conv_s, lru_s, pool_s) = _trunk(
        x_sample.reshape(bs * t1, D_MODEL), layers, final_g, rope_s, 1, tri,
        batch=bs, rows=t1, tq=t1, n_pad=0, topk=min(TOPK_MAX, (p_len + t1) // 4), causal=False,
        frames_start=0, n_hist=p_len, past=past, conv_st=state_conv, lru_st=state_lru, pool_st=state_pool)
    y_sample = y_s.reshape(bs, t1, D_MODEL)
    return (y_prompt, y_sample, k_p, v_p, ki_p, conv_p, lru_p, pool_p, k_s, v_s, ki_s, conv_s, lru_s, pool_s)
```

```python
import functools

import numpy as np
import jax
import jax.numpy as jnp
from jax import lax
from jax.experimental import pallas as pl
from jax.experimental.pallas import tpu as pltpu

F32 = jnp.float32
BF16 = jnp.bfloat16

D_MODEL = 1024
CHUNK = 64
N_META = 16
RMS_EPS = 1e-6
N_HEADS = 8
HEAD_DIM = 64
ATTN_WIDTH = N_HEADS * HEAD_DIM
ROT_DIM = HEAD_DIM // 4
ROPE_THETA = 500000.0
IDX_HEADS = 4
IDX_DIM = 64
TOPK_MAX = 256
LRU_WIDTH = 512
LRU_BLOCKS = 8
LRU_BLOCK_DIM = LRU_WIDTH // LRU_BLOCKS
CONV_WIDTH = 4
LRU_C = 8.0
POOL_WIDTH = 512
POOL_WINDOWS = (2, 4, 8, 16)
POOL_GROUPS = 4
POOL_GROUP_DIM = POOL_WIDTH // POOL_GROUPS
POOL_HIST = 15
N_BRANCH = 3
BRANCH_WIDTH = 512
COL_SIZES = (ATTN_WIDTH, ATTN_WIDTH, ATTN_WIDTH, ATTN_WIDTH, IDX_HEADS * IDX_DIM, IDX_DIM, IDX_HEADS,
             LRU_WIDTH, LRU_WIDTH, POOL_WIDTH, POOL_WIDTH, N_BRANCH * D_MODEL)

LANES = 128
SUBLANES = 8

COL_Q = 0
COL_K = 512
COL_V = 1024
COL_GA = 1536
COL_XB = 2048
COL_GB = 2560
COL_XC = 3072
COL_GC = 3584
COL_GM = 4096
COL_QI = 7168
COL_KIW = 7424
N_PACKED = 7552
PROJ_CHUNK = 512

ROW_TILE = 256
KEY_TILE = 256
SEL_ROWS = 64
SEL_KEYS = 64
ATT_SUB = 256
QK_AHEAD = 4
NEG_INF = float("-inf")
POS_INF = float("inf")
F32_LOWEST = float(np.finfo(np.float32).min)
NEVER_CHUNK = 2 ** 30
MASK_LOGIT = -1e30
M_INIT = -1e29
LOG2_E = 1.4426950408889634
VMEM_LIMIT = 56 * 1024 * 1024


def _cparams(sem):
    return pltpu.CompilerParams(dimension_semantics=sem, vmem_limit_bytes=VMEM_LIMIT)


def _rope_slab(y, c, s1, s2):
    half = ROT_DIM // 2
    return y * c + pltpu.roll(y, LANES - half, 1) * s1 + pltpu.roll(y, half, 1) * s2


def _proj_kernel(x_ref, g_ref, w_ref, rope_ref, main_ref, kb_ref, vb_ref, *, v_transposed):
    x = x_ref[...]
    ms = jnp.mean(x * x, axis=-1, keepdims=True)
    hn = (x * lax.rsqrt(ms + RMS_EPS) * g_ref[...]).astype(BF16)
    c = rope_ref[:, 0:LANES]
    s1 = rope_ref[:, LANES:2 * LANES]
    s2 = rope_ref[:, 2 * LANES:3 * LANES]
    lane = lax.broadcasted_iota(jnp.int32, (x.shape[0], LANES), 1)
    for c0 in range(0, N_PACKED, PROJ_CHUNK):
        cw = min(PROJ_CHUNK, N_PACKED - c0)
        y = jnp.dot(hn, w_ref[:, c0:c0 + cw], preferred_element_type=F32)
        if c0 in (COL_Q, COL_K, COL_QI):
            slabs = []
            for j in range(cw // LANES):
                slab = y[:, j * LANES:(j + 1) * LANES]
                roped = _rope_slab(slab, c, s1, s2)
                if c0 + j * LANES == COL_KIW:
                    roped = jnp.where(lane < IDX_DIM, roped, slab)
                slabs.append(roped)
            y = jnp.concatenate(slabs, axis=1)
        main_ref[:, c0:c0 + cw] = y
        if c0 == COL_K:
            kb_ref[...] = y.astype(BF16)
        if c0 == COL_V:
            vb_ref[...] = (y.T if v_transposed else y).astype(BF16)


def _proj(x2d, g, wp, rope, rope_tiles, v_transposed):
    m = x2d.shape[0]
    tm = ROW_TILE
    if v_transposed:
        v_spec = pl.BlockSpec((ATTN_WIDTH, tm), lambda i: (0, i))
        v_shape = jax.ShapeDtypeStruct((ATTN_WIDTH, m), BF16)
    else:
        v_spec = pl.BlockSpec((tm, ATTN_WIDTH), lambda i: (i, 0))
        v_shape = jax.ShapeDtypeStruct((m, ATTN_WIDTH), BF16)
    return pl.pallas_call(
        functools.partial(_proj_kernel, v_transposed=v_transposed),
        grid=(m // tm,),
        in_specs=[
            pl.BlockSpec((tm, D_MODEL), lambda i: (i, 0)),
            pl.BlockSpec((1, D_MODEL), lambda i: (0, 0)),
            pl.BlockSpec((D_MODEL, N_PACKED), lambda i: (0, 0), pipeline_mode=pl.Buffered(1)),
            pl.BlockSpec((tm, 3 * LANES), lambda i: (i % rope_tiles, 0)),
        ],
        out_specs=[
            pl.BlockSpec((tm, N_PACKED), lambda i: (i, 0)),
            pl.BlockSpec((tm, ATTN_WIDTH), lambda i: (i, 0)),
            v_spec,
        ],
        out_shape=[
            jax.ShapeDtypeStruct((m, N_PACKED), F32),
            jax.ShapeDtypeStruct((m, ATTN_WIDTH), BF16),
            v_shape,
        ],
        compiler_params=_cparams(("arbitrary",)),
        name="proj",
    )(x2d, g, wp, rope)


def _attn_kernel(q_ref, qi_ref, wq_ref, kb_ref, vb_ref, kiw_ref, tri_ref, o_ref,
                 sc_ref, m_ref, l_ref, acc_ref, *, tq, topk, causal, frames_start, k_lo, k_hi, n_kt_total):
    i = pl.program_id(1)
    tk = KEY_TILE
    kf = float(topk)
    n_kt = jnp.minimum(i + 1, n_kt_total) if causal else n_kt_total

    qrow = i * tq + lax.broadcasted_iota(jnp.int32, (tq, 1), 0)
    if causal:
        qchunk = jnp.where(qrow < frames_start, 0, (qrow - frames_start) // CHUNK + 1)

    qi_b = qi_ref[...].astype(BF16)
    w = wq_ref[:, IDX_DIM:IDX_DIM + IDX_HEADS]

    def score_tile(kt, carry):
        n_adm, mn, mx = carry
        k0 = pl.multiple_of(kt * tk, tk)
        ki_t = kiw_ref[pl.ds(k0, tk), 0:IDX_DIM].astype(BF16)
        s = jnp.zeros((tq, tk), F32)
        for h in range(IDX_HEADS):
            d = lax.dot_general(qi_b[:, h * IDX_DIM:(h + 1) * IDX_DIM], ki_t,
                                (((1,), (1,)), ((), ())), preferred_element_type=F32)
            s = s + w[:, h:h + 1] * jnp.maximum(d, 0.0)
        krow = k0 + lax.broadcasted_iota(jnp.int32, (1, tk), 1)
        adm = (krow >= k_lo) & (krow < k_hi)
        if causal:
            kchunk = jnp.where(krow < frames_start, 0, (krow - frames_start) // CHUNK + 1)
            adm = adm & (kchunk <= qchunk)
        s = jnp.where(adm, s, NEG_INF)
        sc_ref[:, pl.ds(k0, tk)] = s
        n_adm = n_adm + jnp.sum(jnp.where(adm, 1.0, 0.0), axis=1, keepdims=True)
        mn = jnp.minimum(mn, jnp.min(jnp.where(adm, s, POS_INF), axis=1, keepdims=True))
        mx = jnp.maximum(mx, jnp.max(s, axis=1, keepdims=True))
        return n_adm, mn, mx

    n_adm, mn, mx = lax.fori_loop(
        0, n_kt, score_tile,
        (jnp.zeros((tq, 1), F32), jnp.full((tq, 1), POS_INF, F32), jnp.full((tq, 1), NEG_INF, F32)))

    few = n_adm <= kf
    lo0 = jnp.where(few, NEG_INF, mn)
    hi0 = jnp.where(few, NEG_INF, mx)
    n_kc = n_kt * (tk // LANES)

    def row_pass(rb, thr):
        thr_b = jnp.broadcast_to(thr, (SEL_ROWS, LANES))

        def body(kc, c):
            cnt, dn, up = c
            s = sc_ref[rb * SEL_ROWS:(rb + 1) * SEL_ROWS, pl.ds(pl.multiple_of(kc * LANES, LANES), LANES)]
            gt = s > thr_b
            cnt = cnt + jnp.where(gt, 1.0, 0.0)
            dn = jnp.maximum(dn, jnp.where(gt, NEG_INF, s))
            up = jnp.minimum(up, jnp.where(gt, s, POS_INF))
            return cnt, dn, up

        cnt, dn, up = lax.fori_loop(
            0, n_kc, body,
            (jnp.zeros((SEL_ROWS, LANES), F32), jnp.full((SEL_ROWS, LANES), NEG_INF, F32),
             jnp.full((SEL_ROWS, LANES), POS_INF, F32)))
        return (jnp.sum(cnt, axis=1, keepdims=True), jnp.max(dn, axis=1, keepdims=True),
                jnp.min(up, axis=1, keepdims=True))

    def search_cond(c):
        lo, hi = c
        return jnp.sum(jnp.where(lo < hi, 1.0, 0.0)) > 0.0

    def search_body(c):
        lo, hi = c
        new_lo, new_hi = [], []
        for rb in range(tq // SEL_ROWS):
            lo_b = lo[rb * SEL_ROWS:(rb + 1) * SEL_ROWS]
            hi_b = hi[rb * SEL_ROWS:(rb + 1) * SEL_ROWS]
            mid = 0.5 * lo_b + 0.5 * hi_b
            mid = jnp.where(mid >= hi_b, lo_b, mid)
            cnt, dn, up = row_pass(rb, mid)
            active = lo_b < hi_b
            ge = cnt >= kf
            new_lo.append(jnp.where(active & ge, up, lo_b))
            new_hi.append(jnp.where(active & jnp.logical_not(ge), dn, hi_b))
        return jnp.concatenate(new_lo, axis=0), jnp.concatenate(new_hi, axis=0)

    thr, _ = lax.while_loop(search_cond, search_body, (lo0, hi0))
    cnt_gt = jnp.concatenate(
        [row_pass(rb, thr[rb * SEL_ROWS:(rb + 1) * SEL_ROWS])[0] for rb in range(tq // SEL_ROWS)], axis=0)
    need = kf - cnt_gt

    q_b = (q_ref[...] * (HEAD_DIM ** -0.5)).astype(BF16)
    m_ref[...] = jnp.full(m_ref.shape, MASK_LOGIT, F32)
    l_ref[...] = jnp.zeros(l_ref.shape, F32)
    acc_ref[...] = jnp.zeros(acc_ref.shape, F32)

    def attend_tile(kt, run):
        k0 = pl.multiple_of(kt * tk, tk)
        s = sc_ref[:, pl.ds(k0, tk)]
        eq = (s == thr) & (s > NEG_INF)
        eqf = jnp.where(eq, 1.0, 0.0)
        rank = run + jnp.dot(eqf.astype(BF16), tri_ref[...], preferred_element_type=F32)
        sel = (s > thr) | (eq & (rank < need))
        run = run + jnp.sum(eqf, axis=1, keepdims=True)
        for h in range(N_HEADS):
            hs = slice(h * HEAD_DIM, (h + 1) * HEAD_DIM)
            lg = lax.dot_general(q_b[:, hs], kb_ref[pl.ds(k0, tk), hs],
                                 (((1,), (1,)), ((), ())), preferred_element_type=F32)
            m_old = m_ref[h][:, 0:1]
            m_new = jnp.maximum(m_old, jnp.max(jnp.where(sel, lg, MASK_LOGIT), axis=1, keepdims=True))
            p = jnp.where(sel, jnp.exp(lg - m_new), 0.0)
            alpha = jnp.exp(m_old - m_new)
            l_new = alpha * l_ref[h][:, 0:1] + jnp.sum(p, axis=1, keepdims=True)
            pv = jnp.dot(p.astype(BF16), vb_ref[pl.ds(k0, tk), hs], preferred_element_type=F32)
            acc_ref[h] = alpha * acc_ref[h] + pv
            m_ref[h] = jnp.broadcast_to(m_new, (tq, LANES))
            l_ref[h] = jnp.broadcast_to(l_new, (tq, LANES))
        return run

    lax.fori_loop(0, n_kt, attend_tile, jnp.zeros((tq, 1), F32))
    o_ref[...] = jnp.concatenate([acc_ref[h] / l_ref[h][:, 0:1] for h in range(N_HEADS)], axis=1)


def _attention(main, kb, vb, kiw, kiw_col, tri, *, batch, tq, topk, causal, frames_start, k_lo, k_hi):
    n_q = main.shape[0] // batch // tq
    s_len = kb.shape[0] // batch
    n_kt_total = s_len // KEY_TILE
    kern = functools.partial(_attn_kernel, tq=tq, topk=topk, causal=causal, frames_start=frames_start,
                             k_lo=k_lo, k_hi=k_hi, n_kt_total=n_kt_total)
    return pl.pallas_call(
        kern,
        grid=(batch, n_q),
        in_specs=[
            pl.BlockSpec((tq, ATTN_WIDTH), lambda b, i: (b * n_q + i, COL_Q // ATTN_WIDTH)),
            pl.BlockSpec((tq, IDX_HEADS * IDX_DIM), lambda b, i: (b * n_q + i, COL_QI // (IDX_HEADS * IDX_DIM))),
            pl.BlockSpec((tq, LANES), lambda b, i: (b * n_q + i, COL_KIW // LANES)),
            pl.BlockSpec((s_len, ATTN_WIDTH), lambda b, i: (b, 0)),
            pl.BlockSpec((s_len, ATTN_WIDTH), lambda b, i: (b, 0)),
            pl.BlockSpec((s_len, LANES), lambda b, i: (b, kiw_col)),
            pl.BlockSpec((KEY_TILE, KEY_TILE), lambda b, i: (0, 0)),
        ],
        out_specs=pl.BlockSpec((tq, ATTN_WIDTH), lambda b, i: (b * n_q + i, 0)),
        out_shape=jax.ShapeDtypeStruct((main.shape[0], ATTN_WIDTH), F32),
        scratch_shapes=[
            pltpu.VMEM((tq, s_len), F32),
            pltpu.VMEM((N_HEADS, tq, LANES), F32),
            pltpu.VMEM((N_HEADS, tq, LANES), F32),
            pltpu.VMEM((N_HEADS, tq, HEAD_DIM), F32),
        ],
        compiler_params=_cparams(("arbitrary", "arbitrary")),
        name="attn",
    )(main, main, main, kb, vb, kiw, tri)


def _attn_t_kernel(q_ref, qi_ref, wq_ref, kb_ref, vt_ref, kiw_ref, tril_ref, o_ref,
                   sc_ref, *acc_refs, tq, topk, frames_start, k_lo, k_hi):
    i = pl.program_id(1)
    tk = KEY_TILE
    kf = float(topk)
    n_kt = i + 1

    qrow = i * tq + lax.broadcasted_iota(jnp.int32, (1, tq), 1)
    qchunk = jnp.where(qrow < frames_start, 0, (qrow - frames_start) // CHUNK + 1)
    qi_b = qi_ref[...].astype(BF16)
    w_t = wq_ref[...].T

    def raw_scores(k0):
        ki_t = kiw_ref[pl.ds(k0, tk), 0:IDX_DIM].astype(BF16)
        s = jnp.zeros((tk, tq), F32)
        for h in range(IDX_HEADS):
            d = lax.dot_general(ki_t, qi_b[:, h * IDX_DIM:(h + 1) * IDX_DIM],
                                (((1,), (1,)), ((), ())), preferred_element_type=F32)
            s = s + w_t[IDX_DIM + h:IDX_DIM + h + 1, :] * jnp.maximum(d, 0.0)
        return s

    def edge_tile(kt, carry):
        n_adm, mn, mx = carry
        k0 = pl.multiple_of(kt * tk, tk)
        krow = k0 + lax.broadcasted_iota(jnp.int32, (tk, 1), 0)
        kchunk = jnp.where(krow < frames_start, 0, (krow - frames_start) // CHUNK + 1)
        kchunk = jnp.where((krow >= k_lo) & (krow < k_hi), kchunk, NEVER_CHUNK)
        adm = kchunk <= qchunk
        s = jnp.where(adm, raw_scores(k0), NEG_INF)
        sc_ref[pl.ds(k0, tk), :] = s
        n_adm = n_adm + jnp.sum(jnp.where(adm, 1.0, 0.0), axis=0, keepdims=True)
        mn = jnp.minimum(mn, jnp.min(jnp.where(adm, s, POS_INF), axis=0, keepdims=True))
        mx = jnp.maximum(mx, jnp.max(s, axis=0, keepdims=True))
        return n_adm, mn, mx

    def inner_tile(kt, carry):
        n_adm, mn, mx = carry
        k0 = pl.multiple_of(kt * tk, tk)
        s = raw_scores(k0)
        sc_ref[pl.ds(k0, tk), :] = s
        return (n_adm + float(tk), jnp.minimum(mn, jnp.min(s, axis=0, keepdims=True)),
                jnp.maximum(mx, jnp.max(s, axis=0, keepdims=True)))

    stats = edge_tile(0, (jnp.zeros((1, tq), F32), jnp.full((1, tq), POS_INF, F32),
                          jnp.full((1, tq), NEG_INF, F32)))
    stats = lax.fori_loop(1, i, inner_tile, stats)
    n_adm, mn, mx = lax.cond(i >= 1, lambda c: edge_tile(i, c), lambda c: c, stats)

    n_steps = n_kt * (tk // SEL_KEYS)

    def key_pass(thr, with_bounds):
        thr_b = jnp.broadcast_to(thr, (SUBLANES, tq))

        def body(step, c):
            cnt, a, b = c
            r0 = pl.multiple_of(step * SEL_KEYS, SEL_KEYS)
            for j in range(SEL_KEYS // SUBLANES):
                s = sc_ref[pl.ds(r0 + j * SUBLANES, SUBLANES), :]
                gt = s > thr_b
                cnt = cnt + jnp.where(gt, 1.0, 0.0)
                if with_bounds:
                    a = jnp.maximum(a, jnp.where(gt, NEG_INF, s))
                    b = jnp.minimum(b, jnp.where(gt, s, POS_INF))
                else:
                    a = a + jnp.where(s == thr_b, 1.0, 0.0)
            return cnt, a, b

        init_a = jnp.full((SUBLANES, tq), NEG_INF, F32) if with_bounds else jnp.zeros((SUBLANES, tq), F32)
        cnt, a, b = lax.fori_loop(0, n_steps, body,
                                  (jnp.zeros((SUBLANES, tq), F32), init_a, jnp.full((SUBLANES, tq), POS_INF, F32)))
        cnt = jnp.sum(cnt, axis=0, keepdims=True)
        if with_bounds:
            return cnt, jnp.max(a, axis=0, keepdims=True), jnp.min(b, axis=0, keepdims=True)
        return cnt, jnp.sum(a, axis=0, keepdims=True)

    few = n_adm <= kf
    lo0 = jnp.where(few, NEG_INF, mn)
    hi0 = jnp.where(few, NEG_INF, mx)

    def search_cond(c):
        lo, hi = c
        return jnp.sum(jnp.where(lo < hi, 1.0, 0.0)) > 0.0

    def search_body(c):
        lo, hi = c
        mid = 0.5 * lo + 0.5 * hi
        mid = jnp.where(mid >= hi, lo, mid)
        cnt, dn, up = key_pass(mid, True)
        active = lo < hi
        ge = cnt >= kf
        return (jnp.where(active & ge, up, lo), jnp.where(active & jnp.logical_not(ge), dn, hi))

    thr, _ = lax.while_loop(search_cond, search_body, (lo0, hi0))
    thr = jnp.where(thr == NEG_INF, F32_LOWEST, thr)
    cnt_gt, cnt_eq = key_pass(thr, False)
    need = kf - cnt_gt
    partial_ties = jnp.sum(jnp.where(cnt_eq > need, 1.0, 0.0)) > 0.0

    q_b = (q_ref[...] * (HEAD_DIM ** -0.5 * LOG2_E)).astype(BF16)
    for acc_ref in acc_refs:
        acc_ref[...] = jnp.zeros(acc_ref.shape, F32)

    def attend_tile(kt, carry):
        run, m, l = carry
        k0 = pl.multiple_of(kt * tk, tk)
        s = sc_ref[pl.ds(k0, tk), :]

        def ranked(run):
            eq = s == thr
            eqf = jnp.where(eq, 1.0, 0.0)
            rank = run + jnp.dot(tril_ref[...], eqf.astype(BF16), preferred_element_type=F32)
            sel = (s > thr) | (eq & (rank < need))
            return jnp.where(sel, 0.0, MASK_LOGIT), run + jnp.sum(eqf, axis=0, keepdims=True)

        def plain(run):
            return jnp.where(s >= thr, 0.0, MASK_LOGIT), run

        bias, run = lax.cond(partial_ties, ranked, plain, run)
        m_rows = [m[h:h + 1, :] for h in range(N_HEADS)]
        l_rows = [l[h:h + 1, :] for h in range(N_HEADS)]
        def logits(sub, h):
            ks = pl.ds(pl.multiple_of(k0 + sub * ATT_SUB, ATT_SUB), ATT_SUB)
            hs = slice(h * HEAD_DIM, (h + 1) * HEAD_DIM)
            return lax.dot_general(kb_ref[ks, hs], q_b[:, hs], (((1,), (1,)), ((), ())),
                                   preferred_element_type=F32)

        order = [(sub, h) for sub in range(tk // ATT_SUB) for h in range(N_HEADS)]
        raws = [logits(*u) for u in order[:QK_AHEAD]]
        pending = None
        for n, (sub, h) in enumerate(order):
            if n + QK_AHEAD < len(order):
                raws.append(logits(*order[n + QK_AHEAD]))
            lg = bias[sub * ATT_SUB:(sub + 1) * ATT_SUB, :] + raws[n]
            m_new = jnp.maximum(m_rows[h], jnp.max(lg, axis=0, keepdims=True))
            p = jnp.exp2(lg - m_new)
            alpha = jnp.exp2(m_rows[h] - m_new)
            l_rows[h] = alpha * l_rows[h] + jnp.sum(p, axis=0, keepdims=True)
            m_rows[h] = m_new
            ks = pl.ds(pl.multiple_of(k0 + sub * ATT_SUB, ATT_SUB), ATT_SUB)
            pv = jnp.dot(vt_ref[h * HEAD_DIM:(h + 1) * HEAD_DIM, ks], p.astype(BF16),
                         preferred_element_type=F32)
            if pending is not None:
                ph, palpha, ppv = pending
                acc_refs[ph][...] = palpha * acc_refs[ph][...] + ppv
            pending = (h, alpha, pv)
        ph, palpha, ppv = pending
        acc_refs[ph][...] = palpha * acc_refs[ph][...] + ppv
        return run, jnp.concatenate(m_rows, axis=0), jnp.concatenate(l_rows, axis=0)

    _, _, l = lax.fori_loop(0, n_kt, attend_tile,
                            (jnp.zeros((1, tq), F32), jnp.full((N_HEADS, tq), M_INIT, F32),
                             jnp.zeros((N_HEADS, tq), F32)))
    out_t = jnp.concatenate([acc_ref[...] / l[h:h + 1, :] for h, acc_ref in enumerate(acc_refs)], axis=0)
    o_ref[...] = out_t.T


def _attention_t(main, kb, vt, tri_lower, *, batch, tq, topk, frames_start, k_lo, k_hi):
    s_len = kb.shape[0] // batch
    n_q = s_len // tq
    assert tq == KEY_TILE and frames_start == KEY_TILE and KEY_TILE % CHUNK == 0
    assert 0 <= k_lo <= KEY_TILE and k_hi == s_len and s_len % KEY_TILE == 0
    kern = functools.partial(_attn_t_kernel, tq=tq, topk=topk, frames_start=frames_start, k_lo=k_lo, k_hi=k_hi)
    return pl.pallas_call(
        kern,
        grid=(batch, n_q),
        in_specs=[
            pl.BlockSpec((tq, ATTN_WIDTH), lambda b, i: (b * n_q + i, COL_Q // ATTN_WIDTH)),
            pl.BlockSpec((tq, IDX_HEADS * IDX_DIM), lambda b, i: (b * n_q + i, COL_QI // (IDX_HEADS * IDX_DIM))),
            pl.BlockSpec((tq, LANES), lambda b, i: (b * n_q + i, COL_KIW // LANES)),
            pl.BlockSpec((s_len, ATTN_WIDTH), lambda b, i: (b, 0)),
            pl.BlockSpec((ATTN_WIDTH, s_len), lambda b, i: (0, b)),
            pl.BlockSpec((s_len, LANES), lambda b, i: (b, COL_KIW // LANES)),
            pl.BlockSpec((KEY_TILE, KEY_TILE), lambda b, i: (0, 0)),
        ],
        out_specs=pl.BlockSpec((tq, ATTN_WIDTH), lambda b, i: (b * n_q + i, 0)),
        out_shape=jax.ShapeDtypeStruct((main.shape[0], ATTN_WIDTH), F32),
        scratch_shapes=[pltpu.VMEM((s_len, tq), F32)] + [pltpu.VMEM((HEAD_DIM, tq), F32)] * N_HEADS,
        compiler_params=_cparams(("arbitrary", "arbitrary")),
        name="attn_t",
    )(main, main, main, kb, vt, main, tri_lower)


def _lru_kernel(xb_ref, cst_ref, h0_ref, cw_ref, cb_ref, wa_ref, ba_ref, wx_ref, bx_ref, lam_ref,
                hb_ref, hlast_ref, xpad_ref, a_ref, b_ref, h_ref, *, tt, n_pad):
    t = pl.program_id(1)
    hist = SUBLANES

    @pl.when(t == 0)
    def _():
        xpad_ref[0:hist, :] = jnp.zeros((hist, LRU_WIDTH), F32)
        xpad_ref[hist - (CONV_WIDTH - 1):hist, :] = cst_ref[0]
        h_ref[...] = h0_ref[0]

    xpad_ref[hist:hist + tt, :] = xb_ref[...]
    xc = jnp.broadcast_to(cb_ref[...], (tt, LRU_WIDTH))
    for j in range(CONV_WIDTH):
        off = hist - (CONV_WIDTH - 1) + j
        xc = xc + xpad_ref[off:off + tt, :] * cw_ref[j:j + 1, :]
    xpad_ref[0:hist, :] = xpad_ref[tt:tt + hist, :]

    xcb = xc.astype(BF16)
    r = jax.nn.sigmoid(jnp.dot(xcb, wa_ref[...], preferred_element_type=F32) + ba_ref[...])
    g = jax.nn.sigmoid(jnp.dot(xcb, wx_ref[...], preferred_element_type=F32) + bx_ref[...])
    log_a = -LRU_C * r * jax.nn.softplus(-lam_ref[...])
    a = jnp.exp(log_a)
    b = jnp.sqrt(1.0 - a * a) * (g * xc)
    if n_pad:
        row = t * tt + lax.broadcasted_iota(jnp.int32, (tt, 1), 0)
        b = jnp.where(row >= n_pad, b, 0.0)
    a_ref[...] = a
    b_ref[...] = b

    def group(gi, h):
        r0 = pl.multiple_of(gi * SUBLANES, SUBLANES)
        ag = a_ref[pl.ds(r0, SUBLANES), :]
        bg = b_ref[pl.ds(r0, SUBLANES), :]
        rows = []
        for j in range(SUBLANES):
            h = ag[j:j + 1, :] * h + bg[j:j + 1, :]
            rows.append(h)
        hb_ref[pl.ds(r0, SUBLANES), :] = jnp.concatenate(rows, axis=0)
        return h

    h = lax.fori_loop(0, tt // SUBLANES, group, h_ref[...])
    h_ref[...] = h
    hlast_ref[0] = h


def _lru(main, conv_st, h0, lw, *, batch, tt, n_pad):
    n_t = main.shape[0] // batch // tt
    kern = functools.partial(_lru_kernel, tt=tt, n_pad=n_pad)
    const = lambda shape: pl.BlockSpec(shape, lambda b, t: (0,) * len(shape))
    return pl.pallas_call(
        kern,
        grid=(batch, n_t),
        in_specs=[
            pl.BlockSpec((tt, LRU_WIDTH), lambda b, t: (b * n_t + t, COL_XB // LRU_WIDTH)),
            pl.BlockSpec((1, CONV_WIDTH - 1, LRU_WIDTH), lambda b, t: (b, 0, 0)),
            pl.BlockSpec((1, 1, LRU_WIDTH), lambda b, t: (b, 0, 0)),
            const((CONV_WIDTH, LRU_WIDTH)), const((1, LRU_WIDTH)),
            const((LRU_WIDTH, LRU_WIDTH)), const((1, LRU_WIDTH)),
            const((LRU_WIDTH, LRU_WIDTH)), const((1, LRU_WIDTH)),
            const((1, LRU_WIDTH)),
        ],
        out_specs=[
            pl.BlockSpec((tt, LRU_WIDTH), lambda b, t: (b * n_t + t, 0)),
            pl.BlockSpec((1, 1, LRU_WIDTH), lambda b, t: (b, 0, 0)),
        ],
        out_shape=[
            jax.ShapeDtypeStruct((main.shape[0], LRU_WIDTH), F32),
            jax.ShapeDtypeStruct((batch, 1, LRU_WIDTH), F32),
        ],
        scratch_shapes=[
            pltpu.VMEM((tt + SUBLANES, LRU_WIDTH), F32),
            pltpu.VMEM((tt, LRU_WIDTH), F32),
            pltpu.VMEM((tt, LRU_WIDTH), F32),
            pltpu.VMEM((1, LRU_WIDTH), F32),
        ],
        compiler_params=_cparams(("arbitrary", "arbitrary")),
        name="lru",
    )(main, conv_st, h0, lw["conv_w"], lw["conv_b"], lw["wa"], lw["ba"], lw["wx"], lw["bx"], lw["lam"])


def _pool_kernel(xc_ref, pst_ref, pw_ref, ps_ref, hc_ref, xpad_ref, *, tt, n_pad, n_hist):
    t = pl.program_id(1)
    hist = 2 * SUBLANES

    @pl.when(t == 0)
    def _():
        xpad_ref[0:hist, :] = jnp.zeros((hist, POOL_WIDTH), F32)
        xpad_ref[hist - POOL_HIST:hist, :] = pst_ref[0]

    x = xc_ref[...]
    xpad_ref[hist:hist + tt, :] = x
    row = t * tt + lax.broadcasted_iota(jnp.int32, (tt, 1), 0)
    seen = jnp.maximum(row - n_pad + 1 + n_hist, 1).astype(F32)
    outs = []
    for gi, win in enumerate(POOL_WINDOWS):
        cs = slice(gi * POOL_GROUP_DIM, (gi + 1) * POOL_GROUP_DIM)
        tot = x[:, cs]
        for j in range(1, win):
            tot = tot + xpad_ref[hist - j:hist - j + tt, cs]
        pooled = tot / jnp.minimum(float(win), seen) - x[:, cs]
        outs.append(jnp.dot(pooled.astype(BF16), pw_ref[gi], preferred_element_type=F32))
    hc_ref[...] = jnp.concatenate(outs, axis=1) * ps_ref[...]
    xpad_ref[0:hist, :] = xpad_ref[tt:tt + hist, :]


def _pool(main, pool_st, pw, ps, *, batch, tt, n_pad, n_hist):
    n_t = main.shape[0] // batch // tt
    kern = functools.partial(_pool_kernel, tt=tt, n_pad=n_pad, n_hist=n_hist)
    return pl.pallas_call(
        kern,
        grid=(batch, n_t),
        in_specs=[
            pl.BlockSpec((tt, POOL_WIDTH), lambda b, t: (b * n_t + t, COL_XC // POOL_WIDTH)),
            pl.BlockSpec((1, POOL_HIST, POOL_WIDTH), lambda b, t: (b, 0, 0)),
            pl.BlockSpec((POOL_GROUPS, POOL_GROUP_DIM, POOL_GROUP_DIM), lambda b, t: (0, 0, 0)),
            pl.BlockSpec((1, POOL_WIDTH), lambda b, t: (0, 0)),
        ],
        out_specs=pl.BlockSpec((tt, POOL_WIDTH), lambda b, t: (b * n_t + t, 0)),
        out_shape=jax.ShapeDtypeStruct((main.shape[0], POOL_WIDTH), F32),
        scratch_shapes=[pltpu.VMEM((tt + 2 * SUBLANES, POOL_WIDTH), F32)],
        compiler_params=_cparams(("arbitrary", "arbitrary")),
        name="pool",
    )(main, pool_st, pw, ps)


def _silu(v):
    return v * jax.nn.sigmoid(v)


def _merge_kernel(x_ref, at_ref, ga_ref, hb_ref, gb_ref, hc_ref, gc_ref, gm0_ref, gm1_ref, gm2_ref,
                  wb_ref, wo_ref, fg_ref, o_ref, *, tm, rows_per_seq, n_pad, final):
    ys = (at_ref[...] * _silu(ga_ref[...]), hb_ref[...] * _silu(gb_ref[...]), hc_ref[...] * _silu(gc_ref[...]))
    merged = jnp.zeros((tm, D_MODEL), F32)
    for n, (y, gm_ref) in enumerate(zip(ys, (gm0_ref, gm1_ref, gm2_ref))):
        pb = jnp.dot(y.astype(BF16), wb_ref[n], preferred_element_type=F32)
        merged = merged + jax.nn.sigmoid(gm_ref[...]) * pb
    out = x_ref[...] + jnp.dot(merged.astype(BF16), wo_ref[...], preferred_element_type=F32)
    if n_pad:
        row = (pl.program_id(0) * tm) % rows_per_seq + lax.broadcasted_iota(jnp.int32, (tm, 1), 0)
        out = jnp.where(row >= n_pad, out, 0.0)
    if final:
        ms = jnp.mean(out * out, axis=-1, keepdims=True)
        out = out * lax.rsqrt(ms + RMS_EPS) * fg_ref[...]
    o_ref[...] = out


def _merge(x2d, attn, hb, hc, main, wb, wo, fg, *, rows_per_seq, n_pad, final):
    m = x2d.shape[0]
    tm = ROW_TILE
    kern = functools.partial(_merge_kernel, tm=tm, rows_per_seq=rows_per_seq, n_pad=n_pad, final=final)
    half = lambda col: pl.BlockSpec((tm, BRANCH_WIDTH), lambda i: (i, col // BRANCH_WIDTH))
    full = lambda col: pl.BlockSpec((tm, D_MODEL), lambda i: (i, col // D_MODEL))
    return pl.pallas_call(
        kern,
        grid=(m // tm,),
        in_specs=[
            full(0), half(0), half(COL_GA), half(0), half(COL_GB), half(0), half(COL_GC),
            full(COL_GM), full(COL_GM + D_MODEL), full(COL_GM + 2 * D_MODEL),
            pl.BlockSpec((N_BRANCH, BRANCH_WIDTH, D_MODEL), lambda i: (0, 0, 0)),
            pl.BlockSpec((D_MODEL, D_MODEL), lambda i: (0, 0)),
            pl.BlockSpec((1, D_MODEL), lambda i: (0, 0)),
        ],
        out_specs=pl.BlockSpec((tm, D_MODEL), lambda i: (i, 0)),
        out_shape=jax.ShapeDtypeStruct((m, D_MODEL), F32),
        compiler_params=_cparams(("arbitrary",)),
        name="merge",
    )(x2d, attn, main, hb, main, hc, main, main, main, main, wb, wo, fg)


def _pack_w_in(w):
    offs = np.cumsum((0,) + COL_SIZES)
    q, k, v, ga, qi, ki, wi, xb, gb, xc, gc, gm = [w[:, offs[j]:offs[j + 1]] for j in range(len(COL_SIZES))]
    pad = jnp.zeros((w.shape[0], LANES - IDX_DIM - IDX_HEADS), w.dtype)
    return jnp.concatenate([q, k, v, ga, xb, gb, xc, gc, gm, qi, ki, wi, pad], axis=1).astype(BF16)


def _block_diag(w):
    nb, c, d = w.shape
    eye = jnp.eye(nb, dtype=w.dtype)
    return (w[:, :, None, :] * eye[:, None, :, None]).reshape(nb * c, nb * d)


def _rope_table(pos):
    half = ROT_DIM // 2
    inv = ROPE_THETA ** (-jnp.arange(0, ROT_DIM, 2, dtype=F32) / ROT_DIM)
    ang = pos.astype(F32)[:, None] * inv[None, :]
    cos, sin = jnp.cos(ang), jnp.sin(ang)
    n = pos.shape[0]
    rest1 = jnp.ones((n, HEAD_DIM - ROT_DIM), F32)
    rest0 = jnp.zeros((n, HEAD_DIM - ROT_DIM), F32)
    zh = jnp.zeros((n, half), F32)
    c = jnp.concatenate([cos, cos, rest1], axis=1)
    s1 = jnp.concatenate([-sin, zh, rest0], axis=1)
    s2 = jnp.concatenate([zh, sin, rest0], axis=1)
    return jnp.concatenate([c, c, s1, s1, s2, s2], axis=1)


def _layer_weights(l, norm_g, w_in, conv_w, conv_b, lru_wa, lru_ba, lru_wx, lru_bx, lru_lambda, pool_w,
                   pool_scale, w_branch_out, w_out):
    row = lambda v: v.reshape(1, -1)
    return dict(
        g=row(norm_g[l]), wp=_pack_w_in(w_in[l]),
        lru=dict(conv_w=conv_w[l], conv_b=row(conv_b[l]), wa=_block_diag(lru_wa[l]).astype(BF16), ba=row(lru_ba[l]),
                 wx=_block_diag(lru_wx[l]).astype(BF16), bx=row(lru_bx[l]), lam=row(lru_lambda[l])),
        pw=pool_w[l].astype(BF16), ps=row(pool_scale[l]),
        wb=w_branch_out[l].astype(BF16), wo=w_out[l].astype(BF16))


def _trunk(x2d, layers, final_g, rope, rope_tiles, *, batch, rows, tq, n_pad, topk, frames_start,
           n_hist, past, conv_st, lru_st, pool_st):
    ids = lax.broadcasted_iota(jnp.int32, (KEY_TILE, KEY_TILE), 0), lax.broadcasted_iota(jnp.int32, (KEY_TILE, KEY_TILE), 1)
    tri_upper = (ids[0] < ids[1]).astype(BF16)
    tri_lower = (ids[0] > ids[1]).astype(BF16)
    depth = len(layers)
    news = []
    for l, lw in enumerate(layers):
        main, kb, vb = _proj(x2d, lw["g"], lw["wp"], rope, rope_tiles, past is None)
        if past is None:
            attn = _attention_t(main, kb, vb, tri_lower, batch=batch, tq=tq, topk=topk, frames_start=frames_start,
                                k_lo=n_pad, k_hi=rows)
        else:
            pk, pv, pki = past[l]
            p_len = pk.shape[1]
            s_len = -(-(p_len + rows) // KEY_TILE) * KEY_TILE
            tail = s_len - p_len - rows

            def cat(p, new, width, dt):
                parts = [p.astype(dt), new.reshape(batch, rows, width).astype(dt)]
                if tail:
                    parts.append(jnp.zeros((batch, tail, width), dt))
                return jnp.concatenate(parts, axis=1).reshape(batch * s_len, width)

            kb_all = cat(pk, kb, ATTN_WIDTH, BF16)
            vb_all = cat(pv, vb, ATTN_WIDTH, BF16)
            pki_w = jnp.pad(pki, ((0, 0), (0, 0), (0, LANES - IDX_DIM)))
            kiw_all, kiw_col = cat(pki_w, main[:, COL_KIW:COL_KIW + LANES], LANES, F32), 0
            attn = _attention(main, kb_all, vb_all, kiw_all, kiw_col, tri_upper, batch=batch, tq=tq, topk=topk,
                              causal=False, frames_start=frames_start, k_lo=0, k_hi=p_len + rows)
        hb, h_last = _lru(main, conv_st[l], lru_st[l].reshape(batch, 1, LRU_WIDTH), lw["lru"],
                          batch=batch, tt=tq, n_pad=n_pad)
        hc = _pool(main, pool_st[l], lw["pw"], lw["ps"], batch=batch, tt=tq, n_pad=n_pad, n_hist=n_hist)
        x2d = _merge(x2d, attn, hb, hc, main, lw["wb"], lw["wo"], final_g, rows_per_seq=rows, n_pad=n_pad,
                     final=(l == depth - 1))
        m3 = main.reshape(batch, rows, N_PACKED)
        real = rows - n_pad
        k_new = m3[:, n_pad:, COL_K:COL_K + ATTN_WIDTH].reshape(batch, real, N_HEADS, HEAD_DIM)
        v_new = m3[:, n_pad:, COL_V:COL_V + ATTN_WIDTH].reshape(batch, real, N_HEADS, HEAD_DIM)
        ki_new = m3[:, n_pad:, COL_KIW:COL_KIW + IDX_DIM]
        xb_seq = jnp.concatenate([conv_st[l], m3[:, n_pad:, COL_XB:COL_XB + LRU_WIDTH][:, -(CONV_WIDTH - 1):]], axis=1)
        xc_seq = jnp.concatenate([pool_st[l], m3[:, n_pad:, COL_XC:COL_XC + POOL_WIDTH][:, -POOL_HIST:]], axis=1)
        news.append((k_new, v_new, ki_new, xb_seq[:, -(CONV_WIDTH - 1):], h_last.reshape(batch, LRU_WIDTH),
                     xc_seq[:, -POOL_HIST:]))
    stacked = [jnp.stack([n[j] for n in news]) for j in range(6)]
    return x2d, stacked


def kernel(x_prompt, x_sample, cache_k, cache_v, cache_kidx, state_conv, state_lru, state_pool, meta_tokens,
           norm_g, w_in, conv_w, conv_b, lru_wa, lru_ba, lru_wx, lru_bx, lru_lambda, pool_w, pool_scale,
           w_branch_out, w_out, final_norm_g):
    depth = w_in.shape[0]
    layers = [_layer_weights(l, norm_g, w_in, conv_w, conv_b, lru_wa, lru_ba, lru_wx, lru_bx, lru_lambda, pool_w,
                             pool_scale, w_branch_out, w_out) for l in range(depth)]
    final_g = final_norm_g.reshape(1, D_MODEL)
    dt = x_prompt.dtype

    bp, seq, _ = x_prompt.shape
    assert seq % ROW_TILE == 0 and N_META <= ROW_TILE
    n_pad = ROW_TILE - N_META
    rows_p = ROW_TILE + seq
    x0 = jnp.concatenate([jnp.zeros((bp, n_pad, D_MODEL), dt),
                          jnp.broadcast_to(meta_tokens.astype(dt)[None], (bp, N_META, D_MODEL)), x_prompt], axis=1)
    pos_p = jnp.maximum(jnp.arange(rows_p, dtype=jnp.int32) - n_pad, 0)
    zeros = lambda *s: jnp.zeros((depth, bp) + s, dt)
    y_full, (k_p, v_p, ki_p, conv_p, lru_p, pool_p) = _trunk(
        x0.reshape(bp * rows_p, D_MODEL), layers, final_g, _rope_table(pos_p), rows_p // ROW_TILE,
        batch=bp, rows=rows_p, tq=ROW_TILE, n_pad=n_pad, topk=min(TOPK_MAX, seq // 4),
        frames_start=ROW_TILE, n_hist=0, past=None, conv_st=zeros(CONV_WIDTH - 1, LRU_WIDTH),
        lru_st=zeros(LRU_WIDTH), pool_st=zeros(POOL_HIST, POOL_WIDTH))
    y_prompt = y_full.reshape(bp, rows_p, D_MODEL)[:, ROW_TILE:]

    bs, t1, _ = x_sample.shape
    p_len = cache_k.shape[2]
    assert ROW_TILE % t1 == 0 and (bs * t1) % ROW_TILE == 0 and t1 % SEL_ROWS == 0
    pos_s = p_len + jnp.arange(t1, dtype=jnp.int32)
    rope_s = jnp.tile(_rope_table(pos_s), (ROW_TILE // t1, 1))
    past = [(cache_k[l].reshape(bs, p_len, ATTN_WIDTH), cache_v[l].reshape(bs, p_len, ATTN_WIDTH), cache_kidx[l])
            for l in range(depth)]
    y_s, (k_s, v_s, ki_s, conv_s, lru_s, pool_s) = _trunk(
        x_sample.reshape(bs * t1, D_MODEL), layers, final_g, rope_s, 1,
        batch=bs, rows=t1, tq=t1, n_pad=0, topk=min(TOPK_MAX, (p_len + t1) // 4),
        frames_start=0, n_hist=p_len, past=past, conv_st=state_conv, lru_st=state_lru, pool_st=state_pool)
    y_sample = y_s.reshape(bs, t1, D_MODEL)
    return (y_prompt, y_sample, k_p, v_p, ki_p, conv_p, lru_p, pool_p, k_s, v_s, ki_s, conv_s, lru_s, pool_s)
```

```python
import functools

import numpy as np
import jax
import jax.numpy as jnp
from jax import lax
from jax.experimental import pallas as pl
from jax.experimental.pallas import tpu as pltpu

F32 = jnp.float32
BF16 = jnp.bfloat16

D_MODEL = 1024
CHUNK = 64
N_META = 16
RMS_EPS = 1e-6
N_HEADS = 8
HEAD_DIM = 64
ATTN_WIDTH = N_HEADS * HEAD_DIM
ROT_DIM = HEAD_DIM // 4
ROPE_THETA = 500000.0
IDX_HEADS = 4
IDX_DIM = 64
TOPK_MAX = 256
LRU_WIDTH = 512
LRU_BLOCKS = 8
LRU_BLOCK_DIM = LRU_WIDTH // LRU_BLOCKS
CONV_WIDTH = 4
LRU_C = 8.0
POOL_WIDTH = 512
POOL_WINDOWS = (2, 4, 8, 16)
POOL_GROUPS = 4
POOL_GROUP_DIM = POOL_WIDTH // POOL_GROUPS
POOL_HIST = 15
N_BRANCH = 3
BRANCH_WIDTH = 512
COL_SIZES = (ATTN_WIDTH, ATTN_WIDTH, ATTN_WIDTH, ATTN_WIDTH, IDX_HEADS * IDX_DIM, IDX_DIM, IDX_HEADS,
             LRU_WIDTH, LRU_WIDTH, POOL_WIDTH, POOL_WIDTH, N_BRANCH * D_MODEL)

LANES = 128
SUBLANES = 8

COL_Q = 0
COL_K = 512
COL_V = 1024
COL_GA = 1536
COL_XB = 2048
COL_GB = 2560
COL_XC = 3072
COL_GC = 3584
COL_GM = 4096
COL_QI = 7168
COL_KIW = 7424
N_PACKED = 7552
PROJ_CHUNK = 512

ROW_TILE = 256
KEY_TILE = 256
CACHE_TILE = 256
SEL_KEYS = 128
COARSE_STEPS = 12
ATT_SUB = 256
QK_AHEAD = 4
NEG_INF = float("-inf")
POS_INF = float("inf")
F32_LOWEST = float(np.finfo(np.float32).min)
NEVER_CHUNK = 2 ** 30
MASK_LOGIT = -1e30
M_INIT = -1e29
LOG2_E = 1.4426950408889634
VMEM_LIMIT = 56 * 1024 * 1024


def _cparams(sem):
    return pltpu.CompilerParams(dimension_semantics=sem, vmem_limit_bytes=VMEM_LIMIT)


def _rope_slab(y, c, s1, s2):
    half = ROT_DIM // 2
    return y * c + pltpu.roll(y, LANES - half, 1) * s1 + pltpu.roll(y, half, 1) * s2


def _proj_kernel(x_ref, g_ref, w_ref, rope_ref, main_ref, kb_ref, vb_ref, *, v_transposed):
    x = x_ref[...]
    ms = jnp.mean(x * x, axis=-1, keepdims=True)
    hn = (x * lax.rsqrt(ms + RMS_EPS) * g_ref[...]).astype(BF16)
    c = rope_ref[:, 0:LANES]
    s1 = rope_ref[:, LANES:2 * LANES]
    s2 = rope_ref[:, 2 * LANES:3 * LANES]
    lane = lax.broadcasted_iota(jnp.int32, (x.shape[0], LANES), 1)
    for c0 in range(0, N_PACKED, PROJ_CHUNK):
        cw = min(PROJ_CHUNK, N_PACKED - c0)
        y = jnp.dot(hn, w_ref[:, c0:c0 + cw], preferred_element_type=F32)
        if c0 in (COL_Q, COL_K, COL_QI):
            slabs = []
            for j in range(cw // LANES):
                slab = y[:, j * LANES:(j + 1) * LANES]
                roped = _rope_slab(slab, c, s1, s2)
                if c0 + j * LANES == COL_KIW:
                    roped = jnp.where(lane < IDX_DIM, roped, slab)
                slabs.append(roped)
            y = jnp.concatenate(slabs, axis=1)
        main_ref[:, c0:c0 + cw] = y
        if c0 == COL_K:
            kb_ref[...] = y.astype(BF16)
        if c0 == COL_V:
            vb_ref[...] = (y.T if v_transposed else y).astype(BF16)


def _proj(x2d, g, wp, rope, rope_tiles, v_transposed):
    m = x2d.shape[0]
    tm = ROW_TILE
    if v_transposed:
        v_spec = pl.BlockSpec((ATTN_WIDTH, tm), lambda i: (0, i))
        v_shape = jax.ShapeDtypeStruct((ATTN_WIDTH, m), BF16)
    else:
        v_spec = pl.BlockSpec((tm, ATTN_WIDTH), lambda i: (i, 0))
        v_shape = jax.ShapeDtypeStruct((m, ATTN_WIDTH), BF16)
    return pl.pallas_call(
        functools.partial(_proj_kernel, v_transposed=v_transposed),
        grid=(m // tm,),
        in_specs=[
            pl.BlockSpec((tm, D_MODEL), lambda i: (i, 0)),
            pl.BlockSpec((1, D_MODEL), lambda i: (0, 0)),
            pl.BlockSpec((D_MODEL, N_PACKED), lambda i: (0, 0), pipeline_mode=pl.Buffered(1)),
            pl.BlockSpec((tm, 3 * LANES), lambda i: (i % rope_tiles, 0)),
        ],
        out_specs=[
            pl.BlockSpec((tm, N_PACKED), lambda i: (i, 0)),
            pl.BlockSpec((tm, ATTN_WIDTH), lambda i: (i, 0)),
            v_spec,
        ],
        out_shape=[
            jax.ShapeDtypeStruct((m, N_PACKED), F32),
            jax.ShapeDtypeStruct((m, ATTN_WIDTH), BF16),
            v_shape,
        ],
        compiler_params=_cparams(("arbitrary",)),
        name="proj",
    )(x2d, g, wp, rope)


def _attn_c_kernel(q_ref, qi_ref, wq_ref, kn_ref, vn_ref, kc_ref, vc_ref, kic_ref, tri_ref, o_ref,
                   sc_ref, thr_ref, need_ref, run_ref, m_ref, l_ref, acc_ref, ties_ref, *, t_new, p_len, topk):
    kt = pl.program_id(1)
    n_ct = p_len // CACHE_TILE
    kf = float(topk)
    n_cols = sc_ref.shape[1]

    @pl.when(kt == 0)
    def _():
        qi_b = qi_ref[...].astype(BF16)
        w = wq_ref[:, IDX_DIM:IDX_DIM + IDX_HEADS]

        def scores(ki):
            s = jnp.zeros((t_new, ki.shape[0]), F32)
            for h in range(IDX_HEADS):
                d = lax.dot_general(qi_b[:, h * IDX_DIM:(h + 1) * IDX_DIM], ki.astype(BF16),
                                    (((1,), (1,)), ((), ())), preferred_element_type=F32)
                s = s + w[:, h:h + 1] * jnp.maximum(d, 0.0)
            return s

        mn = jnp.full((t_new, 1), POS_INF, F32)
        mx = jnp.full((t_new, 1), NEG_INF, F32)
        for c in range(n_ct):
            s = scores(kic_ref[0, c * CACHE_TILE:(c + 1) * CACHE_TILE, :])
            sc_ref[:, c * CACHE_TILE:(c + 1) * CACHE_TILE] = s
            mn = jnp.minimum(mn, jnp.min(s, axis=1, keepdims=True))
            mx = jnp.maximum(mx, jnp.max(s, axis=1, keepdims=True))
        s = scores(wq_ref[:, 0:IDX_DIM])
        mn = jnp.minimum(mn, jnp.min(s, axis=1, keepdims=True))
        mx = jnp.maximum(mx, jnp.max(s, axis=1, keepdims=True))
        sc_ref[:, p_len:n_cols] = jnp.concatenate(
            [s, jnp.full((t_new, n_cols - p_len - t_new), NEG_INF, F32)], axis=1)

        def row_pass(thr, mode):
            thr_b = jnp.broadcast_to(thr, (t_new, LANES))

            def body(kc, c):
                cnt, a, b = c
                s = sc_ref[:, pl.ds(pl.multiple_of(kc * LANES, LANES), LANES)]
                gt = s > thr_b
                cnt = cnt + jnp.where(gt, 1.0, 0.0)
                if mode == "bounds":
                    a = jnp.maximum(a, jnp.where(gt, NEG_INF, s))
                    b = jnp.minimum(b, jnp.where(gt, s, POS_INF))
                elif mode == "ties":
                    a = a + jnp.where(s == thr_b, 1.0, 0.0)
                return cnt, a, b

            init_a = jnp.full((t_new, LANES), NEG_INF, F32) if mode == "bounds" else jnp.zeros((t_new, LANES), F32)
            cnt, a, b = lax.fori_loop(0, n_cols // LANES, body,
                                      (jnp.zeros((t_new, LANES), F32), init_a, jnp.full((t_new, LANES), POS_INF, F32)))
            cnt = jnp.sum(cnt, axis=1, keepdims=True)
            if mode == "bounds":
                return cnt, jnp.max(a, axis=1, keepdims=True), jnp.min(b, axis=1, keepdims=True)
            if mode == "ties":
                return cnt, jnp.sum(a, axis=1, keepdims=True)
            return cnt

        def midpoint(lo, hi):
            mid = 0.5 * lo + 0.5 * hi
            return jnp.where(mid >= hi, lo, mid)

        def coarse_step(_, c):
            lo, hi = c
            mid = midpoint(lo, hi)
            ge = row_pass(mid, "count") >= kf
            active = lo < hi
            return (jnp.where(active & ge, mid, lo), jnp.where(active & jnp.logical_not(ge), mid, hi))

        def search_cond(c):
            lo, hi = c
            return jnp.sum(jnp.where(lo < hi, 1.0, 0.0)) > 0.0

        def search_body(c):
            lo, hi = c
            cnt, dn, up = row_pass(midpoint(lo, hi), "bounds")
            active = lo < hi
            ge = cnt >= kf
            return (jnp.where(active & ge, up, lo), jnp.where(active & jnp.logical_not(ge), dn, hi))

        bracket = lax.fori_loop(0, COARSE_STEPS, coarse_step, (mn, mx))
        thr, _ = lax.while_loop(search_cond, search_body, bracket)
        cnt_gt, cnt_eq = row_pass(thr, "ties")
        need = kf - cnt_gt
        thr_ref[...] = jnp.broadcast_to(thr, (t_new, LANES))
        need_ref[...] = jnp.broadcast_to(need, (t_new, LANES))
        ties_ref[0] = (jnp.sum(jnp.where(cnt_eq > need, 1.0, 0.0)) > 0.0).astype(jnp.int32)
        run_ref[...] = jnp.zeros(run_ref.shape, F32)
        m_ref[...] = jnp.full(m_ref.shape, M_INIT, F32)
        l_ref[...] = jnp.zeros(l_ref.shape, F32)
        acc_ref[...] = jnp.zeros(acc_ref.shape, F32)

    q_b = (q_ref[...] * (HEAD_DIM ** -0.5 * LOG2_E)).astype(BF16)
    thr = thr_ref[:, 0:1]
    need = need_ref[:, 0:1]

    def attend(s, k_of, v_of):
        n = s.shape[1]

        def ranked():
            eq = s == thr
            eqf = jnp.where(eq, 1.0, 0.0)
            run = run_ref[:, 0:1]
            rank = run + jnp.dot(eqf.astype(BF16), tri_ref[0:n, 0:n], preferred_element_type=F32)
            run_ref[...] = jnp.broadcast_to(run + jnp.sum(eqf, axis=1, keepdims=True), run_ref.shape)
            return jnp.where((s > thr) | (eq & (rank < need)), 0.0, MASK_LOGIT)

        bias = lax.cond(ties_ref[0] > 0, ranked, lambda: jnp.where(s >= thr, 0.0, MASK_LOGIT))
        logits = lambda h: lax.dot_general(q_b[:, h * HEAD_DIM:(h + 1) * HEAD_DIM], k_of(h),
                                           (((1,), (1,)), ((), ())), preferred_element_type=F32)
        raws = [logits(h) for h in range(QK_AHEAD)]
        for h in range(N_HEADS):
            if h + QK_AHEAD < N_HEADS:
                raws.append(logits(h + QK_AHEAD))
            lg = bias + raws[h]
            m_old = m_ref[h][:, 0:1]
            m_new = jnp.maximum(m_old, jnp.max(lg, axis=1, keepdims=True))
            p = jnp.exp2(lg - m_new)
            alpha = jnp.exp2(m_old - m_new)
            l_ref[h] = jnp.broadcast_to(alpha * l_ref[h][:, 0:1] + jnp.sum(p, axis=1, keepdims=True), (t_new, LANES))
            m_ref[h] = jnp.broadcast_to(m_new, (t_new, LANES))
            acc_ref[h] = alpha * acc_ref[h] + jnp.dot(p.astype(BF16), v_of(h), preferred_element_type=F32)

    @pl.when(kt < n_ct)
    def _():
        heads = lambda ref: (lambda h: ref[0, pl.ds(h, CACHE_TILE, stride=N_HEADS), :].astype(BF16))
        attend(sc_ref[:, pl.ds(pl.multiple_of(kt * CACHE_TILE, CACHE_TILE), CACHE_TILE)], heads(kc_ref), heads(vc_ref))

    @pl.when(kt == n_ct)
    def _():
        cols = lambda ref: (lambda h: ref[:, h * HEAD_DIM:(h + 1) * HEAD_DIM].astype(BF16))
        attend(sc_ref[:, p_len:p_len + t_new], cols(kn_ref), cols(vn_ref))
        o_ref[...] = jnp.concatenate([acc_ref[h] / l_ref[h][:, 0:1] for h in range(N_HEADS)], axis=1)


def _attention_c(main, cache_k, cache_v, cache_kidx, layer, tri_upper, *, batch, t_new, topk):
    p_len = cache_kidx.shape[1]
    n_ct = p_len // CACHE_TILE
    assert p_len % CACHE_TILE == 0 and t_new <= CACHE_TILE and p_len + t_new > topk
    n_cols = p_len + -(-t_new // LANES) * LANES
    kern = functools.partial(_attn_c_kernel, t_new=t_new, p_len=p_len, topk=topk)
    new = lambda width, col: pl.BlockSpec((t_new, width), lambda b, kt: (b, col // width))
    cached = pl.BlockSpec((1, CACHE_TILE * N_HEADS, HEAD_DIM),
                          lambda b, kt: (layer * batch + b, jnp.minimum(kt, n_ct - 1), 0))
    return pl.pallas_call(
        kern,
        grid=(batch, n_ct + 1),
        in_specs=[
            new(ATTN_WIDTH, COL_Q), new(IDX_HEADS * IDX_DIM, COL_QI), new(LANES, COL_KIW),
            new(ATTN_WIDTH, COL_K), new(ATTN_WIDTH, COL_V),
            cached, cached,
            pl.BlockSpec((1, p_len, IDX_DIM), lambda b, kt: (layer * batch + b, 0, 0)),
            pl.BlockSpec((CACHE_TILE, CACHE_TILE), lambda b, kt: (0, 0)),
        ],
        out_specs=pl.BlockSpec((t_new, ATTN_WIDTH), lambda b, kt: (b, 0)),
        out_shape=jax.ShapeDtypeStruct((main.shape[0], ATTN_WIDTH), F32),
        scratch_shapes=[
            pltpu.VMEM((t_new, n_cols), F32),
            pltpu.VMEM((t_new, LANES), F32),
            pltpu.VMEM((t_new, LANES), F32),
            pltpu.VMEM((t_new, LANES), F32),
            pltpu.VMEM((N_HEADS, t_new, LANES), F32),
            pltpu.VMEM((N_HEADS, t_new, LANES), F32),
            pltpu.VMEM((N_HEADS, t_new, HEAD_DIM), F32),
            pltpu.SMEM((1,), jnp.int32),
        ],
        compiler_params=_cparams(("arbitrary", "arbitrary")),
        name="attn_c",
    )(main, main, main, main, main, cache_k, cache_v, cache_kidx, tri_upper)


def _attn_t_kernel(q_ref, qi_ref, wq_ref, kb_ref, vt_ref, kiw_ref, tril_ref, o_ref,
                   sc_ref, *acc_refs, tq, topk, frames_start, k_lo, k_hi):
    i = pl.program_id(1)
    tk = KEY_TILE
    kf = float(topk)
    n_kt = i + 1

    qrow = i * tq + lax.broadcasted_iota(jnp.int32, (1, tq), 1)
    qchunk = jnp.where(qrow < frames_start, 0, (qrow - frames_start) // CHUNK + 1)
    qi_b = qi_ref[...].astype(BF16)
    w_t = wq_ref[...].T

    def raw_scores(k0):
        ki_t = kiw_ref[pl.ds(k0, tk), 0:IDX_DIM].astype(BF16)
        s = jnp.zeros((tk, tq), F32)
        for h in range(IDX_HEADS):
            d = lax.dot_general(ki_t, qi_b[:, h * IDX_DIM:(h + 1) * IDX_DIM],
                                (((1,), (1,)), ((), ())), preferred_element_type=F32)
            s = s + w_t[IDX_DIM + h:IDX_DIM + h + 1, :] * jnp.maximum(d, 0.0)
        return s

    def edge_tile(kt, carry):
        n_adm, mn, mx = carry
        k0 = pl.multiple_of(kt * tk, tk)
        krow = k0 + lax.broadcasted_iota(jnp.int32, (tk, 1), 0)
        kchunk = jnp.where(krow < frames_start, 0, (krow - frames_start) // CHUNK + 1)
        kchunk = jnp.where((krow >= k_lo) & (krow < k_hi), kchunk, NEVER_CHUNK)
        adm = kchunk <= qchunk
        s = jnp.where(adm, raw_scores(k0), NEG_INF)
        sc_ref[pl.ds(k0, tk), :] = s
        n_adm = n_adm + jnp.sum(jnp.where(adm, 1.0, 0.0), axis=0, keepdims=True)
        mn = jnp.minimum(mn, jnp.min(jnp.where(adm, s, POS_INF), axis=0, keepdims=True))
        mx = jnp.maximum(mx, jnp.max(s, axis=0, keepdims=True))
        return n_adm, mn, mx

    def inner_tile(kt, carry):
        n_adm, mn, mx = carry
        k0 = pl.multiple_of(kt * tk, tk)
        s = raw_scores(k0)
        sc_ref[pl.ds(k0, tk), :] = s
        return (n_adm + float(tk), jnp.minimum(mn, jnp.min(s, axis=0, keepdims=True)),
                jnp.maximum(mx, jnp.max(s, axis=0, keepdims=True)))

    stats = edge_tile(0, (jnp.zeros((1, tq), F32), jnp.full((1, tq), POS_INF, F32),
                          jnp.full((1, tq), NEG_INF, F32)))
    stats = lax.fori_loop(1, i, inner_tile, stats)
    n_adm, mn, mx = lax.cond(i >= 1, lambda c: edge_tile(i, c), lambda c: c, stats)

    def key_pass(thr, mode):
        thr_b = jnp.broadcast_to(thr, (SUBLANES, tq))

        def tree(op, xs):
            while len(xs) > 1:
                xs = [op(xs[k], xs[k + 1]) for k in range(0, len(xs) - 1, 2)] + xs[len(xs) & ~1:]
            return xs[0]

        step_rows = SEL_KEYS if mode == "count" else SEL_KEYS // 2

        def body(step, c):
            cnt, a, b = c
            slab = sc_ref[pl.ds(pl.multiple_of(step * step_rows, step_rows), step_rows), :]
            rows = [slab[j * SUBLANES:(j + 1) * SUBLANES, :] for j in range(step_rows // SUBLANES)]
            gts = [s > thr_b for s in rows]
            cnt = cnt + tree(jnp.add, [jnp.where(gt, 1.0, 0.0) for gt in gts])
            if mode == "bounds":
                a = jnp.maximum(a, tree(jnp.maximum, [jnp.where(gt, NEG_INF, s) for gt, s in zip(gts, rows)]))
                b = jnp.minimum(b, tree(jnp.minimum, [jnp.where(gt, s, POS_INF) for gt, s in zip(gts, rows)]))
            elif mode == "ties":
                a = a + tree(jnp.add, [jnp.where(s == thr_b, 1.0, 0.0) for s in rows])
            return cnt, a, b

        init_a = jnp.full((SUBLANES, tq), NEG_INF, F32) if mode == "bounds" else jnp.zeros((SUBLANES, tq), F32)
        cnt, a, b = lax.fori_loop(0, n_kt * (tk // step_rows), body,
                                  (jnp.zeros((SUBLANES, tq), F32), init_a, jnp.full((SUBLANES, tq), POS_INF, F32)))
        cnt = jnp.sum(cnt, axis=0, keepdims=True)
        if mode == "bounds":
            return cnt, jnp.max(a, axis=0, keepdims=True), jnp.min(b, axis=0, keepdims=True)
        if mode == "ties":
            return cnt, jnp.sum(a, axis=0, keepdims=True)
        return cnt

    few = n_adm <= kf
    lo0 = jnp.where(few, NEG_INF, mn)
    hi0 = jnp.where(few, NEG_INF, mx)

    def midpoint(lo, hi):
        mid = 0.5 * lo + 0.5 * hi
        return jnp.where(mid >= hi, lo, mid)

    def coarse_step(_, c):
        lo, hi = c
        mid = midpoint(lo, hi)
        ge = key_pass(mid, "count") >= kf
        active = lo < hi
        return (jnp.where(active & ge, mid, lo), jnp.where(active & jnp.logical_not(ge), mid, hi))

    def search_cond(c):
        lo, hi = c
        return jnp.sum(jnp.where(lo < hi, 1.0, 0.0)) > 0.0

    def search_body(c):
        lo, hi = c
        cnt, dn, up = key_pass(midpoint(lo, hi), "bounds")
        active = lo < hi
        ge = cnt >= kf
        return (jnp.where(active & ge, up, lo), jnp.where(active & jnp.logical_not(ge), dn, hi))

    bracket = lax.fori_loop(0, COARSE_STEPS, coarse_step, (lo0, hi0))
    thr, _ = lax.while_loop(search_cond, search_body, bracket)
    thr = jnp.where(thr == NEG_INF, F32_LOWEST, thr)
    cnt_gt, cnt_eq = key_pass(thr, "ties")
    need = kf - cnt_gt
    partial_ties = jnp.sum(jnp.where(cnt_eq > need, 1.0, 0.0)) > 0.0

    q_b = (q_ref[...] * (HEAD_DIM ** -0.5 * LOG2_E)).astype(BF16)
    for acc_ref in acc_refs:
        acc_ref[...] = jnp.zeros(acc_ref.shape, F32)

    def attend_tile(kt, carry):
        run, m, l = carry
        k0 = pl.multiple_of(kt * tk, tk)
        s = sc_ref[pl.ds(k0, tk), :]

        def ranked(run):
            eq = s == thr
            eqf = jnp.where(eq, 1.0, 0.0)
            rank = run + jnp.dot(tril_ref[...], eqf.astype(BF16), preferred_element_type=F32)
            sel = (s > thr) | (eq & (rank < need))
            return jnp.where(sel, 0.0, MASK_LOGIT), run + jnp.sum(eqf, axis=0, keepdims=True)

        def plain(run):
            return jnp.where(s >= thr, 0.0, MASK_LOGIT), run

        bias, run = lax.cond(partial_ties, ranked, plain, run)
        m_rows = [m[h:h + 1, :] for h in range(N_HEADS)]
        l_rows = [l[h:h + 1, :] for h in range(N_HEADS)]
        def logits(sub, h):
            ks = pl.ds(pl.multiple_of(k0 + sub * ATT_SUB, ATT_SUB), ATT_SUB)
            hs = slice(h * HEAD_DIM, (h + 1) * HEAD_DIM)
            return lax.dot_general(kb_ref[ks, hs], q_b[:, hs], (((1,), (1,)), ((), ())),
                                   preferred_element_type=F32)

        order = [(sub, h) for sub in range(tk // ATT_SUB) for h in range(N_HEADS)]
        raws = [logits(*u) for u in order[:QK_AHEAD]]
        pending = None
        for n, (sub, h) in enumerate(order):
            if n + QK_AHEAD < len(order):
                raws.append(logits(*order[n + QK_AHEAD]))
            lg = bias[sub * ATT_SUB:(sub + 1) * ATT_SUB, :] + raws[n]
            m_new = jnp.maximum(m_rows[h], jnp.max(lg, axis=0, keepdims=True))
            p = jnp.exp2(lg - m_new)
            alpha = jnp.exp2(m_rows[h] - m_new)
            l_rows[h] = alpha * l_rows[h] + jnp.sum(p, axis=0, keepdims=True)
            m_rows[h] = m_new
            ks = pl.ds(pl.multiple_of(k0 + sub * ATT_SUB, ATT_SUB), ATT_SUB)
            pv = jnp.dot(vt_ref[h * HEAD_DIM:(h + 1) * HEAD_DIM, ks], p.astype(BF16),
                         preferred_element_type=F32)
            if pending is not None:
                ph, palpha, ppv = pending
                acc_refs[ph][...] = palpha * acc_refs[ph][...] + ppv
            pending = (h, alpha, pv)
        ph, palpha, ppv = pending
        acc_refs[ph][...] = palpha * acc_refs[ph][...] + ppv
        return run, jnp.concatenate(m_rows, axis=0), jnp.concatenate(l_rows, axis=0)

    _, _, l = lax.fori_loop(0, n_kt, attend_tile,
                            (jnp.zeros((1, tq), F32), jnp.full((N_HEADS, tq), M_INIT, F32),
                             jnp.zeros((N_HEADS, tq), F32)))
    out_t = jnp.concatenate([acc_ref[...] / l[h:h + 1, :] for h, acc_ref in enumerate(acc_refs)], axis=0)
    o_ref[...] = out_t.T


def _attention_t(main, kb, vt, tri_lower, *, batch, tq, topk, frames_start, k_lo, k_hi):
    s_len = kb.shape[0] // batch
    n_q = s_len // tq
    assert tq == KEY_TILE and frames_start == KEY_TILE and KEY_TILE % CHUNK == 0
    assert 0 <= k_lo <= KEY_TILE and k_hi == s_len and s_len % KEY_TILE == 0
    kern = functools.partial(_attn_t_kernel, tq=tq, topk=topk, frames_start=frames_start, k_lo=k_lo, k_hi=k_hi)
    return pl.pallas_call(
        kern,
        grid=(batch, n_q),
        in_specs=[
            pl.BlockSpec((tq, ATTN_WIDTH), lambda b, i: (b * n_q + i, COL_Q // ATTN_WIDTH)),
            pl.BlockSpec((tq, IDX_HEADS * IDX_DIM), lambda b, i: (b * n_q + i, COL_QI // (IDX_HEADS * IDX_DIM))),
            pl.BlockSpec((tq, LANES), lambda b, i: (b * n_q + i, COL_KIW // LANES)),
            pl.BlockSpec((s_len, ATTN_WIDTH), lambda b, i: (b, 0)),
            pl.BlockSpec((ATTN_WIDTH, s_len), lambda b, i: (0, b)),
            pl.BlockSpec((s_len, LANES), lambda b, i: (b, COL_KIW // LANES)),
            pl.BlockSpec((KEY_TILE, KEY_TILE), lambda b, i: (0, 0)),
        ],
        out_specs=pl.BlockSpec((tq, ATTN_WIDTH), lambda b, i: (b * n_q + i, 0)),
        out_shape=jax.ShapeDtypeStruct((main.shape[0], ATTN_WIDTH), F32),
        scratch_shapes=[pltpu.VMEM((s_len, tq), F32)] + [pltpu.VMEM((HEAD_DIM, tq), F32)] * N_HEADS,
        compiler_params=_cparams(("arbitrary", "arbitrary")),
        name="attn_t",
    )(main, main, main, kb, vt, main, tri_lower)


def _lru_kernel(xb_ref, cst_ref, h0_ref, cw_ref, cb_ref, wa_ref, ba_ref, wx_ref, bx_ref, lam_ref,
                hb_ref, hlast_ref, xpad_ref, a_ref, b_ref, h_ref, *, tt, n_pad):
    t = pl.program_id(1)
    hist = SUBLANES

    @pl.when(t == 0)
    def _():
        xpad_ref[0:hist, :] = jnp.zeros((hist, LRU_WIDTH), F32)
        xpad_ref[hist - (CONV_WIDTH - 1):hist, :] = cst_ref[0]
        h_ref[...] = h0_ref[0]

    xpad_ref[hist:hist + tt, :] = xb_ref[...]
    xc = jnp.broadcast_to(cb_ref[...], (tt, LRU_WIDTH))
    for j in range(CONV_WIDTH):
        off = hist - (CONV_WIDTH - 1) + j
        xc = xc + xpad_ref[off:off + tt, :] * cw_ref[j:j + 1, :]
    xpad_ref[0:hist, :] = xpad_ref[tt:tt + hist, :]

    xcb = xc.astype(BF16)
    r = jax.nn.sigmoid(jnp.dot(xcb, wa_ref[...], preferred_element_type=F32) + ba_ref[...])
    g = jax.nn.sigmoid(jnp.dot(xcb, wx_ref[...], preferred_element_type=F32) + bx_ref[...])
    log_a = -LRU_C * r * jax.nn.softplus(-lam_ref[...])
    a = jnp.exp(log_a)
    b = jnp.sqrt(1.0 - a * a) * (g * xc)
    if n_pad:
        row = t * tt + lax.broadcasted_iota(jnp.int32, (tt, 1), 0)
        b = jnp.where(row >= n_pad, b, 0.0)
    a_ref[...] = a
    b_ref[...] = b

    def group(gi, h):
        r0 = pl.multiple_of(gi * SUBLANES, SUBLANES)
        ag = a_ref[pl.ds(r0, SUBLANES), :]
        bg = b_ref[pl.ds(r0, SUBLANES), :]
        rows = []
        for j in range(SUBLANES):
            h = ag[j:j + 1, :] * h + bg[j:j + 1, :]
            rows.append(h)
        hb_ref[pl.ds(r0, SUBLANES), :] = jnp.concatenate(rows, axis=0)
        return h

    h = lax.fori_loop(0, tt // SUBLANES, group, h_ref[...])
    h_ref[...] = h
    hlast_ref[0] = h


def _lru(main, conv_st, h0, lw, *, batch, tt, n_pad):
    n_t = main.shape[0] // batch // tt
    kern = functools.partial(_lru_kernel, tt=tt, n_pad=n_pad)
    const = lambda shape: pl.BlockSpec(shape, lambda b, t: (0,) * len(shape))
    return pl.pallas_call(
        kern,
        grid=(batch, n_t),
        in_specs=[
            pl.BlockSpec((tt, LRU_WIDTH), lambda b, t: (b * n_t + t, COL_XB // LRU_WIDTH)),
            pl.BlockSpec((1, CONV_WIDTH - 1, LRU_WIDTH), lambda b, t: (b, 0, 0)),
            pl.BlockSpec((1, 1, LRU_WIDTH), lambda b, t: (b, 0, 0)),
            const((CONV_WIDTH, LRU_WIDTH)), const((1, LRU_WIDTH)),
            const((LRU_WIDTH, LRU_WIDTH)), const((1, LRU_WIDTH)),
            const((LRU_WIDTH, LRU_WIDTH)), const((1, LRU_WIDTH)),
            const((1, LRU_WIDTH)),
        ],
        out_specs=[
            pl.BlockSpec((tt, LRU_WIDTH), lambda b, t: (b * n_t + t, 0)),
            pl.BlockSpec((1, 1, LRU_WIDTH), lambda b, t: (b, 0, 0)),
        ],
        out_shape=[
            jax.ShapeDtypeStruct((main.shape[0], LRU_WIDTH), F32),
            jax.ShapeDtypeStruct((batch, 1, LRU_WIDTH), F32),
        ],
        scratch_shapes=[
            pltpu.VMEM((tt + SUBLANES, LRU_WIDTH), F32),
            pltpu.VMEM((tt, LRU_WIDTH), F32),
            pltpu.VMEM((tt, LRU_WIDTH), F32),
            pltpu.VMEM((1, LRU_WIDTH), F32),
        ],
        compiler_params=_cparams(("arbitrary", "arbitrary")),
        name="lru",
    )(main, conv_st, h0, lw["conv_w"], lw["conv_b"], lw["wa"], lw["ba"], lw["wx"], lw["bx"], lw["lam"])


def _pool_kernel(xc_ref, pst_ref, pw_ref, ps_ref, hc_ref, xpad_ref, *, tt, n_pad, n_hist):
    t = pl.program_id(1)
    hist = 2 * SUBLANES

    @pl.when(t == 0)
    def _():
        xpad_ref[0:hist, :] = jnp.zeros((hist, POOL_WIDTH), F32)
        xpad_ref[hist - POOL_HIST:hist, :] = pst_ref[0]

    x = xc_ref[...]
    xpad_ref[hist:hist + tt, :] = x
    row = t * tt + lax.broadcasted_iota(jnp.int32, (tt, 1), 0)
    seen = jnp.maximum(row - n_pad + 1 + n_hist, 1).astype(F32)
    outs = []
    for gi, win in enumerate(POOL_WINDOWS):
        cs = slice(gi * POOL_GROUP_DIM, (gi + 1) * POOL_GROUP_DIM)
        tot = x[:, cs]
        for j in range(1, win):
            tot = tot + xpad_ref[hist - j:hist - j + tt, cs]
        pooled = tot / jnp.minimum(float(win), seen) - x[:, cs]
        outs.append(jnp.dot(pooled.astype(BF16), pw_ref[gi], preferred_element_type=F32))
    hc_ref[...] = jnp.concatenate(outs, axis=1) * ps_ref[...]
    xpad_ref[0:hist, :] = xpad_ref[tt:tt + hist, :]


def _pool(main, pool_st, pw, ps, *, batch, tt, n_pad, n_hist):
    n_t = main.shape[0] // batch // tt
    kern = functools.partial(_pool_kernel, tt=tt, n_pad=n_pad, n_hist=n_hist)
    return pl.pallas_call(
        kern,
        grid=(batch, n_t),
        in_specs=[
            pl.BlockSpec((tt, POOL_WIDTH), lambda b, t: (b * n_t + t, COL_XC // POOL_WIDTH)),
            pl.BlockSpec((1, POOL_HIST, POOL_WIDTH), lambda b, t: (b, 0, 0)),
            pl.BlockSpec((POOL_GROUPS, POOL_GROUP_DIM, POOL_GROUP_DIM), lambda b, t: (0, 0, 0)),
            pl.BlockSpec((1, POOL_WIDTH), lambda b, t: (0, 0)),
        ],
        out_specs=pl.BlockSpec((tt, POOL_WIDTH), lambda b, t: (b * n_t + t, 0)),
        out_shape=jax.ShapeDtypeStruct((main.shape[0], POOL_WIDTH), F32),
        scratch_shapes=[pltpu.VMEM((tt + 2 * SUBLANES, POOL_WIDTH), F32)],
        compiler_params=_cparams(("arbitrary", "arbitrary")),
        name="pool",
    )(main, pool_st, pw, ps)


def _silu(v):
    return v * jax.nn.sigmoid(v)


def _merge_kernel(x_ref, at_ref, ga_ref, hb_ref, gb_ref, hc_ref, gc_ref, gm0_ref, gm1_ref, gm2_ref,
                  wb_ref, wo_ref, fg_ref, o_ref, *, tm, rows_per_seq, n_pad, final):
    ys = (at_ref[...] * _silu(ga_ref[...]), hb_ref[...] * _silu(gb_ref[...]), hc_ref[...] * _silu(gc_ref[...]))
    merged = jnp.zeros((tm, D_MODEL), F32)
    for n, (y, gm_ref) in enumerate(zip(ys, (gm0_ref, gm1_ref, gm2_ref))):
        pb = jnp.dot(y.astype(BF16), wb_ref[n], preferred_element_type=F32)
        merged = merged + jax.nn.sigmoid(gm_ref[...]) * pb
    out = x_ref[...] + jnp.dot(merged.astype(BF16), wo_ref[...], preferred_element_type=F32)
    if n_pad:
        row = (pl.program_id(0) * tm) % rows_per_seq + lax.broadcasted_iota(jnp.int32, (tm, 1), 0)
        out = jnp.where(row >= n_pad, out, 0.0)
    if final:
        ms = jnp.mean(out * out, axis=-1, keepdims=True)
        out = out * lax.rsqrt(ms + RMS_EPS) * fg_ref[...]
    o_ref[...] = out


def _merge(x2d, attn, hb, hc, main, wb, wo, fg, *, rows_per_seq, n_pad, final, skip_tiles=0):
    tm = ROW_TILE
    if skip_tiles:
        assert rows_per_seq % tm == 0 and n_pad <= skip_tiles * tm
        tiles_in = rows_per_seq // tm
        tiles_out = tiles_in - skip_tiles
        m = x2d.shape[0] // tiles_in * tiles_out
        src = lambda i: i // tiles_out * tiles_in + skip_tiles + i % tiles_out
    else:
        m = x2d.shape[0]
        src = lambda i: i
    kern = functools.partial(_merge_kernel, tm=tm, rows_per_seq=rows_per_seq, n_pad=0 if skip_tiles else n_pad,
                             final=final)
    half = lambda col: pl.BlockSpec((tm, BRANCH_WIDTH), lambda i: (src(i), col // BRANCH_WIDTH))
    full = lambda col: pl.BlockSpec((tm, D_MODEL), lambda i: (src(i), col // D_MODEL))
    return pl.pallas_call(
        kern,
        grid=(m // tm,),
        in_specs=[
            full(0), half(0), half(COL_GA), half(0), half(COL_GB), half(0), half(COL_GC),
            full(COL_GM), full(COL_GM + D_MODEL), full(COL_GM + 2 * D_MODEL),
            pl.BlockSpec((N_BRANCH, BRANCH_WIDTH, D_MODEL), lambda i: (0, 0, 0)),
            pl.BlockSpec((D_MODEL, D_MODEL), lambda i: (0, 0)),
            pl.BlockSpec((1, D_MODEL), lambda i: (0, 0)),
        ],
        out_specs=pl.BlockSpec((tm, D_MODEL), lambda i: (i, 0)),
        out_shape=jax.ShapeDtypeStruct((m, D_MODEL), F32),
        compiler_params=_cparams(("arbitrary",)),
        name="merge",
    )(x2d, attn, main, hb, main, hc, main, main, main, main, wb, wo, fg)


def _pack_w_in(w):
    offs = np.cumsum((0,) + COL_SIZES)
    q, k, v, ga, qi, ki, wi, xb, gb, xc, gc, gm = [w[:, offs[j]:offs[j + 1]] for j in range(len(COL_SIZES))]
    pad = jnp.zeros((w.shape[0], LANES - IDX_DIM - IDX_HEADS), w.dtype)
    return jnp.concatenate([q, k, v, ga, xb, gb, xc, gc, gm, qi, ki, wi, pad], axis=1).astype(BF16)


def _block_diag(w):
    nb, c, d = w.shape
    eye = jnp.eye(nb, dtype=w.dtype)
    return (w[:, :, None, :] * eye[:, None, :, None]).reshape(nb * c, nb * d)


def _rope_table(pos):
    half = ROT_DIM // 2
    inv = ROPE_THETA ** (-jnp.arange(0, ROT_DIM, 2, dtype=F32) / ROT_DIM)
    ang = pos.astype(F32)[:, None] * inv[None, :]
    cos, sin = jnp.cos(ang), jnp.sin(ang)
    n = pos.shape[0]
    rest1 = jnp.ones((n, HEAD_DIM - ROT_DIM), F32)
    rest0 = jnp.zeros((n, HEAD_DIM - ROT_DIM), F32)
    zh = jnp.zeros((n, half), F32)
    c = jnp.concatenate([cos, cos, rest1], axis=1)
    s1 = jnp.concatenate([-sin, zh, rest0], axis=1)
    s2 = jnp.concatenate([zh, sin, rest0], axis=1)
    return jnp.concatenate([c, c, s1, s1, s2, s2], axis=1)


def _layer_weights(l, norm_g, w_in, conv_w, conv_b, lru_wa, lru_ba, lru_wx, lru_bx, lru_lambda, pool_w,
                   pool_scale, w_branch_out, w_out):
    row = lambda v: v.reshape(1, -1)
    return dict(
        g=row(norm_g[l]), wp=_pack_w_in(w_in[l]),
        lru=dict(conv_w=conv_w[l], conv_b=row(conv_b[l]), wa=_block_diag(lru_wa[l]).astype(BF16), ba=row(lru_ba[l]),
                 wx=_block_diag(lru_wx[l]).astype(BF16), bx=row(lru_bx[l]), lam=row(lru_lambda[l])),
        pw=pool_w[l].astype(BF16), ps=row(pool_scale[l]),
        wb=w_branch_out[l].astype(BF16), wo=w_out[l].astype(BF16))


def _trunk(x2d, layers, final_g, rope, rope_tiles, *, batch, rows, tq, n_pad, topk, frames_start,
           n_hist, past, conv_st, lru_st, pool_st, out_skip_tiles=0):
    ids = lax.broadcasted_iota(jnp.int32, (KEY_TILE, KEY_TILE), 0), lax.broadcasted_iota(jnp.int32, (KEY_TILE, KEY_TILE), 1)
    tri_upper = (ids[0] < ids[1]).astype(BF16)
    tri_lower = (ids[0] > ids[1]).astype(BF16)
    depth = len(layers)
    news = []
    for l, lw in enumerate(layers):
        main, kb, vb = _proj(x2d, lw["g"], lw["wp"], rope, rope_tiles, past is None)
        if past is None:
            attn = _attention_t(main, kb, vb, tri_lower, batch=batch, tq=tq, topk=topk, frames_start=frames_start,
                                k_lo=n_pad, k_hi=rows)
        else:
            attn = _attention_c(main, *past, l, tri_upper, batch=batch, t_new=rows, topk=topk)
        hb, h_last = _lru(main, conv_st[l], lru_st[l].reshape(batch, 1, LRU_WIDTH), lw["lru"],
                          batch=batch, tt=tq, n_pad=n_pad)
        hc = _pool(main, pool_st[l], lw["pw"], lw["ps"], batch=batch, tt=tq, n_pad=n_pad, n_hist=n_hist)
        last = l == depth - 1
        x2d = _merge(x2d, attn, hb, hc, main, lw["wb"], lw["wo"], final_g, rows_per_seq=rows, n_pad=n_pad,
                     final=last, skip_tiles=out_skip_tiles if last else 0)
        m3 = main.reshape(batch, rows, N_PACKED)
        real = rows - n_pad
        k_new = m3[:, n_pad:, COL_K:COL_K + ATTN_WIDTH].reshape(batch, real, N_HEADS, HEAD_DIM)
        v_new = m3[:, n_pad:, COL_V:COL_V + ATTN_WIDTH].reshape(batch, real, N_HEADS, HEAD_DIM)
        ki_new = m3[:, n_pad:, COL_KIW:COL_KIW + IDX_DIM]
        xb_seq = jnp.concatenate([conv_st[l], m3[:, n_pad:, COL_XB:COL_XB + LRU_WIDTH][:, -(CONV_WIDTH - 1):]], axis=1)
        xc_seq = jnp.concatenate([pool_st[l], m3[:, n_pad:, COL_XC:COL_XC + POOL_WIDTH][:, -POOL_HIST:]], axis=1)
        news.append((k_new, v_new, ki_new, xb_seq[:, -(CONV_WIDTH - 1):], h_last.reshape(batch, LRU_WIDTH),
                     xc_seq[:, -POOL_HIST:]))
    stacked = [jnp.stack([n[j] for n in news]) for j in range(6)]
    return x2d, stacked


def kernel(x_prompt, x_sample, cache_k, cache_v, cache_kidx, state_conv, state_lru, state_pool, meta_tokens,
           norm_g, w_in, conv_w, conv_b, lru_wa, lru_ba, lru_wx, lru_bx, lru_lambda, pool_w, pool_scale,
           w_branch_out, w_out, final_norm_g):
    depth = w_in.shape[0]
    layers = [_layer_weights(l, norm_g, w_in, conv_w, conv_b, lru_wa, lru_ba, lru_wx, lru_bx, lru_lambda, pool_w,
                             pool_scale, w_branch_out, w_out) for l in range(depth)]
    final_g = final_norm_g.reshape(1, D_MODEL)
    dt = x_prompt.dtype

    bp, seq, _ = x_prompt.shape
    assert seq % ROW_TILE == 0 and N_META <= ROW_TILE
    n_pad = ROW_TILE - N_META
    rows_p = ROW_TILE + seq
    x0 = jnp.concatenate([jnp.zeros((bp, n_pad, D_MODEL), dt),
                          jnp.broadcast_to(meta_tokens.astype(dt)[None], (bp, N_META, D_MODEL)), x_prompt], axis=1)
    pos_p = jnp.maximum(jnp.arange(rows_p, dtype=jnp.int32) - n_pad, 0)
    zeros = lambda *s: jnp.zeros((depth, bp) + s, dt)
    y_full, (k_p, v_p, ki_p, conv_p, lru_p, pool_p) = _trunk(
        x0.reshape(bp * rows_p, D_MODEL), layers, final_g, _rope_table(pos_p), rows_p // ROW_TILE,
        batch=bp, rows=rows_p, tq=ROW_TILE, n_pad=n_pad, topk=min(TOPK_MAX, seq // 4),
        frames_start=ROW_TILE, n_hist=0, past=None, conv_st=zeros(CONV_WIDTH - 1, LRU_WIDTH),
        lru_st=zeros(LRU_WIDTH), pool_st=zeros(POOL_HIST, POOL_WIDTH), out_skip_tiles=1)
    y_prompt = y_full.reshape(bp, seq, D_MODEL)

    bs, t1, _ = x_sample.shape
    p_len = cache_k.shape[2]
    assert ROW_TILE % t1 == 0 and (bs * t1) % ROW_TILE == 0 and t1 % SUBLANES == 0
    pos_s = p_len + jnp.arange(t1, dtype=jnp.int32)
    rope_s = jnp.tile(_rope_table(pos_s), (ROW_TILE // t1, 1))
    past = (cache_k.reshape(depth * bs, p_len * N_HEADS, HEAD_DIM), cache_v.reshape(depth * bs, p_len * N_HEADS, HEAD_DIM),
            cache_kidx.reshape(depth * bs, p_len, IDX_DIM))
    y_s, (k_s, v_s, ki_s, conv_s, lru_s, pool_s) = _trunk(
        x_sample.reshape(bs * t1, D_MODEL), layers, final_g, rope_s, 1,
        batch=bs, rows=t1, tq=t1, n_pad=0, topk=min(TOPK_MAX, (p_len + t1) // 4),
        frames_start=0, n_hist=p_len, past=past, conv_st=state_conv, lru_st=state_lru, pool_st=state_pool)
    y_sample = y_s.reshape(bs, t1, D_MODEL)
    return (y_prompt, y_sample, k_p, v_p, ki_p, conv_p, lru_p, pool_p, k_s, v_s, ki_s, conv_s, lru_s, pool_s)
```

```python
import functools

import numpy as np
import jax
import jax.numpy as jnp
from jax import lax
from jax.experimental import pallas as pl
from jax.experimental.pallas import tpu as pltpu

F32 = jnp.float32
BF16 = jnp.bfloat16

D_MODEL = 1024
CHUNK = 64
N_META = 16
RMS_EPS = 1e-6
N_HEADS = 8
HEAD_DIM = 64
ATTN_WIDTH = N_HEADS * HEAD_DIM
ROT_DIM = HEAD_DIM // 4
ROPE_THETA = 500000.0
IDX_HEADS = 4
IDX_DIM = 64
TOPK_MAX = 256
LRU_WIDTH = 512
LRU_BLOCKS = 8
LRU_BLOCK_DIM = LRU_WIDTH // LRU_BLOCKS
CONV_WIDTH = 4
LRU_C = 8.0
POOL_WIDTH = 512
POOL_WINDOWS = (2, 4, 8, 16)
POOL_GROUPS = 4
POOL_GROUP_DIM = POOL_WIDTH // POOL_GROUPS
POOL_HIST = 15
N_BRANCH = 3
BRANCH_WIDTH = 512
COL_SIZES = (ATTN_WIDTH, ATTN_WIDTH, ATTN_WIDTH, ATTN_WIDTH, IDX_HEADS * IDX_DIM, IDX_DIM, IDX_HEADS,
             LRU_WIDTH, LRU_WIDTH, POOL_WIDTH, POOL_WIDTH, N_BRANCH * D_MODEL)

LANES = 128
SUBLANES = 8

COL_Q = 0
COL_K = 512
COL_V = 1024
COL_GA = 1536
COL_XB = 2048
COL_GB = 2560
COL_XC = 3072
COL_GC = 3584
COL_GM = 4096
COL_QI = 7168
COL_KIW = 7424
N_PACKED = 7552
PROJ_CHUNK = 512

ROW_TILE = 256
KEY_TILE = 256
CACHE_TILE = 256
SEL_KEYS = 128
COARSE_STEPS = 12
ATT_SUB = 256
QK_AHEAD = 6
QK_AHEAD_CACHED = 2
NEG_INF = float("-inf")
POS_INF = float("inf")
F32_LOWEST = float(np.finfo(np.float32).min)
NEVER_CHUNK = 2 ** 30
MASK_LOGIT = -1e30
M_INIT = -1e29
_Q_SCALE = HEAD_DIM ** -0.5 * 1.4426950408889634
Q_SCALE_HI = float(np.float32(_Q_SCALE))
Q_SCALE_LO = float(np.float32(_Q_SCALE - Q_SCALE_HI))


def _scaled_query(q):
    return (q * Q_SCALE_HI + q * Q_SCALE_LO).astype(BF16)
VMEM_LIMIT = 56 * 1024 * 1024


def _cparams(sem):
    return pltpu.CompilerParams(dimension_semantics=sem, vmem_limit_bytes=VMEM_LIMIT)


def _rope_slab(y, c, s1, s2):
    half = ROT_DIM // 2
    return y * c + pltpu.roll(y, LANES - half, 1) * s1 + pltpu.roll(y, half, 1) * s2


def _proj_kernel(x_ref, g_ref, w_ref, rope_ref, main_ref, kb_ref, vb_ref, *, v_transposed):
    x = x_ref[...]
    ms = jnp.mean(x * x, axis=-1, keepdims=True)
    hn = (x * lax.rsqrt(ms + RMS_EPS) * g_ref[...]).astype(BF16)
    c = rope_ref[:, 0:LANES]
    s1 = rope_ref[:, LANES:2 * LANES]
    s2 = rope_ref[:, 2 * LANES:3 * LANES]
    lane = lax.broadcasted_iota(jnp.int32, (x.shape[0], LANES), 1)
    for c0 in range(0, N_PACKED, PROJ_CHUNK):
        cw = min(PROJ_CHUNK, N_PACKED - c0)
        y = jnp.dot(hn, w_ref[:, c0:c0 + cw], preferred_element_type=F32)
        if c0 in (COL_Q, COL_K, COL_QI):
            slabs = []
            for j in range(cw // LANES):
                slab = y[:, j * LANES:(j + 1) * LANES]
                roped = _rope_slab(slab, c, s1, s2)
                if c0 + j * LANES == COL_KIW:
                    roped = jnp.where(lane < IDX_DIM, roped, slab)
                slabs.append(roped)
            y = jnp.concatenate(slabs, axis=1)
        main_ref[:, c0:c0 + cw] = y
        if c0 == COL_K:
            kb_ref[...] = y.astype(BF16)
        if c0 == COL_V:
            vb_ref[...] = (y.T if v_transposed else y).astype(BF16)


def _proj(x2d, g, wp, rope, rope_tiles, v_transposed):
    m = x2d.shape[0]
    tm = ROW_TILE
    if v_transposed:
        v_spec = pl.BlockSpec((ATTN_WIDTH, tm), lambda i: (0, i))
        v_shape = jax.ShapeDtypeStruct((ATTN_WIDTH, m), BF16)
    else:
        v_spec = pl.BlockSpec((tm, ATTN_WIDTH), lambda i: (i, 0))
        v_shape = jax.ShapeDtypeStruct((m, ATTN_WIDTH), BF16)
    return pl.pallas_call(
        functools.partial(_proj_kernel, v_transposed=v_transposed),
        grid=(m // tm,),
        in_specs=[
            pl.BlockSpec((tm, D_MODEL), lambda i: (i, 0)),
            pl.BlockSpec((1, D_MODEL), lambda i: (0, 0)),
            pl.BlockSpec((D_MODEL, N_PACKED), lambda i: (0, 0), pipeline_mode=pl.Buffered(1)),
            pl.BlockSpec((tm, 3 * LANES), lambda i: (i % rope_tiles, 0)),
        ],
        out_specs=[
            pl.BlockSpec((tm, N_PACKED), lambda i: (i, 0)),
            pl.BlockSpec((tm, ATTN_WIDTH), lambda i: (i, 0)),
            v_spec,
        ],
        out_shape=[
            jax.ShapeDtypeStruct((m, N_PACKED), F32),
            jax.ShapeDtypeStruct((m, ATTN_WIDTH), BF16),
            v_shape,
        ],
        compiler_params=_cparams(("arbitrary",)),
        name="proj",
    )(x2d, g, wp, rope)


def _attn_c_kernel(q_ref, qi_ref, wq_ref, kn_ref, vn_ref, kc_ref, vc_ref, kic_ref, tri_ref, o_ref,
                   sc_ref, thr_ref, need_ref, run_ref, m_ref, l_ref, acc_ref, ties_ref, *, t_new, p_len, topk):
    kt = pl.program_id(1)
    n_ct = p_len // CACHE_TILE
    kf = float(topk)
    n_cols = sc_ref.shape[1]

    @pl.when(kt == 0)
    def _():
        qi_b = qi_ref[...].astype(BF16)
        w = wq_ref[:, IDX_DIM:IDX_DIM + IDX_HEADS]

        def scores(ki_t):
            s = jnp.zeros((t_new, ki_t.shape[1]), F32)
            for h in range(IDX_HEADS):
                d = jnp.dot(qi_b[:, h * IDX_DIM:(h + 1) * IDX_DIM], ki_t.astype(BF16), preferred_element_type=F32)
                s = s + w[:, h:h + 1] * jnp.maximum(d, 0.0)
            return s

        mn = jnp.full((t_new, 1), POS_INF, F32)
        mx = jnp.full((t_new, 1), NEG_INF, F32)
        for c in range(n_ct):
            s = scores(kic_ref[0, :, c * CACHE_TILE:(c + 1) * CACHE_TILE])
            sc_ref[:, c * CACHE_TILE:(c + 1) * CACHE_TILE] = s
            mn = jnp.minimum(mn, jnp.min(s, axis=1, keepdims=True))
            mx = jnp.maximum(mx, jnp.max(s, axis=1, keepdims=True))
        s = scores(wq_ref[:, 0:IDX_DIM].T)
        mn = jnp.minimum(mn, jnp.min(s, axis=1, keepdims=True))
        mx = jnp.maximum(mx, jnp.max(s, axis=1, keepdims=True))
        sc_ref[:, p_len:n_cols] = jnp.concatenate(
            [s, jnp.full((t_new, n_cols - p_len - t_new), NEG_INF, F32)], axis=1)

        def row_pass(thr, mode):
            thr_b = jnp.broadcast_to(thr, (t_new, LANES))

            def body(kc, c):
                cnt, a, b = c
                s = sc_ref[:, pl.ds(pl.multiple_of(kc * LANES, LANES), LANES)]
                gt = s > thr_b
                cnt = cnt + jnp.where(gt, 1.0, 0.0)
                if mode == "bounds":
                    a = jnp.maximum(a, jnp.where(gt, NEG_INF, s))
                    b = jnp.minimum(b, jnp.where(gt, s, POS_INF))
                elif mode == "ties":
                    a = a + jnp.where(s == thr_b, 1.0, 0.0)
                return cnt, a, b

            init_a = jnp.full((t_new, LANES), NEG_INF, F32) if mode == "bounds" else jnp.zeros((t_new, LANES), F32)
            cnt, a, b = lax.fori_loop(0, n_cols // LANES, body,
                                      (jnp.zeros((t_new, LANES), F32), init_a, jnp.full((t_new, LANES), POS_INF, F32)))
            cnt = jnp.sum(cnt, axis=1, keepdims=True)
            if mode == "bounds":
                return cnt, jnp.max(a, axis=1, keepdims=True), jnp.min(b, axis=1, keepdims=True)
            if mode == "ties":
                return cnt, jnp.sum(a, axis=1, keepdims=True)
            return cnt

        def midpoint(lo, hi):
            mid = 0.5 * lo + 0.5 * hi
            return jnp.where(mid >= hi, lo, mid)

        def coarse_step(_, c):
            lo, hi = c
            mid = midpoint(lo, hi)
            ge = row_pass(mid, "count") >= kf
            active = lo < hi
            return (jnp.where(active & ge, mid, lo), jnp.where(active & jnp.logical_not(ge), mid, hi))

        def search_cond(c):
            lo, hi = c
            return jnp.sum(jnp.where(lo < hi, 1.0, 0.0)) > 0.0

        def search_body(c):
            lo, hi = c
            cnt, dn, up = row_pass(midpoint(lo, hi), "bounds")
            active = lo < hi
            ge = cnt >= kf
            return (jnp.where(active & ge, up, lo), jnp.where(active & jnp.logical_not(ge), dn, hi))

        bracket = lax.fori_loop(0, COARSE_STEPS, coarse_step, (mn, mx))
        thr, _ = lax.while_loop(search_cond, search_body, bracket)
        cnt_gt, cnt_eq = row_pass(thr, "ties")
        need = kf - cnt_gt
        thr_ref[...] = jnp.broadcast_to(thr, (t_new, LANES))
        need_ref[...] = jnp.broadcast_to(need, (t_new, LANES))
        ties_ref[0] = (jnp.sum(jnp.where(cnt_eq > need, 1.0, 0.0)) > 0.0).astype(jnp.int32)
        run_ref[...] = jnp.zeros(run_ref.shape, F32)
        m_ref[...] = jnp.full(m_ref.shape, M_INIT, F32)
        l_ref[...] = jnp.zeros(l_ref.shape, F32)
        acc_ref[...] = jnp.zeros(acc_ref.shape, F32)

    q_b = _scaled_query(q_ref[...])
    thr = thr_ref[:, 0:1]
    need = need_ref[:, 0:1]

    def attend(s, k_of, v_of, feature_major):
        n = s.shape[1]
        k_contract, v_contract = ((0,), (1,)) if feature_major else ((1,), (0,))

        def ranked():
            eq = s == thr
            eqf = jnp.where(eq, 1.0, 0.0)
            run = run_ref[:, 0:1]
            rank = run + jnp.dot(eqf.astype(BF16), tri_ref[0:n, 0:n], preferred_element_type=F32)
            run_ref[...] = jnp.broadcast_to(run + jnp.sum(eqf, axis=1, keepdims=True), run_ref.shape)
            return jnp.where((s > thr) | (eq & (rank < need)), 0.0, MASK_LOGIT)

        bias = lax.cond(ties_ref[0] > 0, ranked, lambda: jnp.where(s >= thr, 0.0, MASK_LOGIT))
        logits = lambda h: lax.dot_general(q_b[:, h * HEAD_DIM:(h + 1) * HEAD_DIM], k_of(h),
                                           (((1,), k_contract), ((), ())), preferred_element_type=F32)
        raws = [logits(h) for h in range(QK_AHEAD_CACHED)]
        for h in range(N_HEADS):
            if h + QK_AHEAD_CACHED < N_HEADS:
                raws.append(logits(h + QK_AHEAD_CACHED))
            lg = bias + raws[h]
            m_old = m_ref[h][:, 0:1]
            m_new = jnp.maximum(m_old, jnp.max(lg, axis=1, keepdims=True))
            p = jnp.exp2(lg - m_new)
            alpha = jnp.exp2(m_old - m_new)
            l_ref[h] = jnp.broadcast_to(alpha * l_ref[h][:, 0:1] + jnp.sum(p, axis=1, keepdims=True), (t_new, LANES))
            m_ref[h] = jnp.broadcast_to(m_new, (t_new, LANES))
            pv = lax.dot_general(p.astype(BF16), v_of(h), (((1,), v_contract), ((), ())), preferred_element_type=F32)
            acc_ref[h] = alpha * acc_ref[h] + pv

    @pl.when(kt < n_ct)
    def _():
        heads = lambda ref: (lambda h: ref[0, h * HEAD_DIM:(h + 1) * HEAD_DIM, :].astype(BF16))
        attend(sc_ref[:, pl.ds(pl.multiple_of(kt * CACHE_TILE, CACHE_TILE), CACHE_TILE)], heads(kc_ref), heads(vc_ref),
               True)

    @pl.when(kt == n_ct)
    def _():
        cols = lambda ref: (lambda h: ref[:, h * HEAD_DIM:(h + 1) * HEAD_DIM].astype(BF16))
        attend(sc_ref[:, p_len:p_len + t_new], cols(kn_ref), cols(vn_ref), False)
        o_ref[...] = jnp.concatenate([acc_ref[h] / l_ref[h][:, 0:1] for h in range(N_HEADS)], axis=1)


def _attention_c(main, cache_k, cache_v, cache_kidx, layer, tri_upper, *, batch, t_new, topk):
    p_len = cache_kidx.shape[2]
    n_ct = p_len // CACHE_TILE
    assert p_len % CACHE_TILE == 0 and t_new <= CACHE_TILE and p_len + t_new > topk
    n_cols = p_len + -(-t_new // LANES) * LANES
    kern = functools.partial(_attn_c_kernel, t_new=t_new, p_len=p_len, topk=topk)
    new = lambda width, col: pl.BlockSpec((t_new, width), lambda b, kt: (b, col // width))
    cached = pl.BlockSpec((1, ATTN_WIDTH, CACHE_TILE),
                          lambda b, kt: (layer * batch + b, 0, jnp.minimum(kt, n_ct - 1)))
    return pl.pallas_call(
        kern,
        grid=(batch, n_ct + 1),
        in_specs=[
            new(ATTN_WIDTH, COL_Q), new(IDX_HEADS * IDX_DIM, COL_QI), new(LANES, COL_KIW),
            new(ATTN_WIDTH, COL_K), new(ATTN_WIDTH, COL_V),
            cached, cached,
            pl.BlockSpec((1, IDX_DIM, p_len), lambda b, kt: (layer * batch + b, 0, 0)),
            pl.BlockSpec((CACHE_TILE, CACHE_TILE), lambda b, kt: (0, 0)),
        ],
        out_specs=pl.BlockSpec((t_new, ATTN_WIDTH), lambda b, kt: (b, 0)),
        out_shape=jax.ShapeDtypeStruct((main.shape[0], ATTN_WIDTH), F32),
        scratch_shapes=[
            pltpu.VMEM((t_new, n_cols), F32),
            pltpu.VMEM((t_new, LANES), F32),
            pltpu.VMEM((t_new, LANES), F32),
            pltpu.VMEM((t_new, LANES), F32),
            pltpu.VMEM((N_HEADS, t_new, LANES), F32),
            pltpu.VMEM((N_HEADS, t_new, LANES), F32),
            pltpu.VMEM((N_HEADS, t_new, HEAD_DIM), F32),
            pltpu.SMEM((1,), jnp.int32),
        ],
        compiler_params=_cparams(("arbitrary", "arbitrary")),
        name="attn_c",
    )(main, main, main, main, main, cache_k, cache_v, cache_kidx, tri_upper)


def _attn_t_kernel(q_ref, qi_ref, wq_ref, kb_ref, vt_ref, kiw_ref, tril_ref, o_ref,
                   sc_ref, *acc_refs, tq, topk, frames_start, k_lo, k_hi):
    i = pl.program_id(1)
    tk = KEY_TILE
    kf = float(topk)
    n_kt = i + 1

    qrow = i * tq + lax.broadcasted_iota(jnp.int32, (1, tq), 1)
    qchunk = jnp.where(qrow < frames_start, 0, (qrow - frames_start) // CHUNK + 1)
    qi_t = qi_ref[...].T.astype(BF16)
    w_t = wq_ref[...].T

    def raw_scores(k0):
        ki_t = kiw_ref[pl.ds(k0, tk), 0:IDX_DIM].astype(BF16)
        s = jnp.zeros((tk, tq), F32)
        for h in range(IDX_HEADS):
            d = jnp.dot(ki_t, qi_t[h * IDX_DIM:(h + 1) * IDX_DIM, :], preferred_element_type=F32)
            s = s + w_t[IDX_DIM + h:IDX_DIM + h + 1, :] * jnp.maximum(d, 0.0)
        return s

    def edge_tile(kt, carry):
        n_adm, mn, mx = carry
        k0 = pl.multiple_of(kt * tk, tk)
        krow = k0 + lax.broadcasted_iota(jnp.int32, (tk, 1), 0)
        kchunk = jnp.where(krow < frames_start, 0, (krow - frames_start) // CHUNK + 1)
        kchunk = jnp.where((krow >= k_lo) & (krow < k_hi), kchunk, NEVER_CHUNK)
        adm = kchunk <= qchunk
        s = jnp.where(adm, raw_scores(k0), NEG_INF)
        sc_ref[pl.ds(k0, tk), :] = s
        n_adm = n_adm + jnp.sum(jnp.where(adm, 1.0, 0.0), axis=0, keepdims=True)
        mn = jnp.minimum(mn, jnp.min(jnp.where(adm, s, POS_INF), axis=0, keepdims=True))
        mx = jnp.maximum(mx, jnp.max(s, axis=0, keepdims=True))
        return n_adm, mn, mx

    def inner_tile(kt, carry):
        n_adm, mn, mx = carry
        k0 = pl.multiple_of(kt * tk, tk)
        s = raw_scores(k0)
        sc_ref[pl.ds(k0, tk), :] = s
        return (n_adm + float(tk), jnp.minimum(mn, jnp.min(s, axis=0, keepdims=True)),
                jnp.maximum(mx, jnp.max(s, axis=0, keepdims=True)))

    stats = edge_tile(0, (jnp.zeros((1, tq), F32), jnp.full((1, tq), POS_INF, F32),
                          jnp.full((1, tq), NEG_INF, F32)))
    stats = lax.fori_loop(1, i, inner_tile, stats)
    n_adm, mn, mx = lax.cond(i >= 1, lambda c: edge_tile(i, c), lambda c: c, stats)

    def key_pass(thr, mode):
        thr_b = jnp.broadcast_to(thr, (SUBLANES, tq))

        def tree(op, xs):
            while len(xs) > 1:
                xs = [op(xs[k], xs[k + 1]) for k in range(0, len(xs) - 1, 2)] + xs[len(xs) & ~1:]
            return xs[0]

        step_rows = SEL_KEYS if mode == "count" else SEL_KEYS // 2

        def body(step, c):
            cnt, a, b = c
            slab = sc_ref[pl.ds(pl.multiple_of(step * step_rows, step_rows), step_rows), :]
            rows = [slab[j * SUBLANES:(j + 1) * SUBLANES, :] for j in range(step_rows // SUBLANES)]
            gts = [s > thr_b for s in rows]
            cnt = cnt + tree(jnp.add, [jnp.where(gt, 1.0, 0.0) for gt in gts])
            if mode == "bounds":
                a = jnp.maximum(a, tree(jnp.maximum, [jnp.where(gt, NEG_INF, s) for gt, s in zip(gts, rows)]))
                b = jnp.minimum(b, tree(jnp.minimum, [jnp.where(gt, s, POS_INF) for gt, s in zip(gts, rows)]))
            elif mode == "ties":
                a = a + tree(jnp.add, [jnp.where(s == thr_b, 1.0, 0.0) for s in rows])
            return cnt, a, b

        init_a = jnp.full((SUBLANES, tq), NEG_INF, F32) if mode == "bounds" else jnp.zeros((SUBLANES, tq), F32)
        cnt, a, b = lax.fori_loop(0, n_kt * (tk // step_rows), body,
                                  (jnp.zeros((SUBLANES, tq), F32), init_a, jnp.full((SUBLANES, tq), POS_INF, F32)))
        cnt = jnp.sum(cnt, axis=0, keepdims=True)
        if mode == "bounds":
            return cnt, jnp.max(a, axis=0, keepdims=True), jnp.min(b, axis=0, keepdims=True)
        if mode == "ties":
            return cnt, jnp.sum(a, axis=0, keepdims=True)
        return cnt

    few = n_adm <= kf
    lo0 = jnp.where(few, NEG_INF, mn)
    hi0 = jnp.where(few, NEG_INF, mx)

    def midpoint(lo, hi):
        mid = 0.5 * lo + 0.5 * hi
        return jnp.where(mid >= hi, lo, mid)

    def coarse_step(_, c):
        lo, hi = c
        mid = midpoint(lo, hi)
        ge = key_pass(mid, "count") >= kf
        active = lo < hi
        return (jnp.where(active & ge, mid, lo), jnp.where(active & jnp.logical_not(ge), mid, hi))

    def search_cond(c):
        lo, hi = c
        return jnp.sum(jnp.where(lo < hi, 1.0, 0.0)) > 0.0

    def search_body(c):
        lo, hi = c
        cnt, dn, up = key_pass(midpoint(lo, hi), "bounds")
        active = lo < hi
        ge = cnt >= kf
        return (jnp.where(active & ge, up, lo), jnp.where(active & jnp.logical_not(ge), dn, hi))

    bracket = lax.fori_loop(0, COARSE_STEPS, coarse_step, (lo0, hi0))
    thr, _ = lax.while_loop(search_cond, search_body, bracket)
    thr = jnp.where(thr == NEG_INF, F32_LOWEST, thr)
    cnt_gt, cnt_eq = key_pass(thr, "ties")
    need = kf - cnt_gt
    partial_ties = jnp.sum(jnp.where(cnt_eq > need, 1.0, 0.0)) > 0.0

    q_t = _scaled_query(q_ref[...].T)
    for acc_ref in acc_refs:
        acc_ref[...] = jnp.zeros(acc_ref.shape, F32)

    def attend_tile(kt, carry):
        run, m, l = carry
        k0 = pl.multiple_of(kt * tk, tk)
        s = sc_ref[pl.ds(k0, tk), :]

        def ranked(run):
            eq = s == thr
            eqf = jnp.where(eq, 1.0, 0.0)
            rank = run + jnp.dot(tril_ref[...], eqf.astype(BF16), preferred_element_type=F32)
            sel = (s > thr) | (eq & (rank < need))
            return jnp.where(sel, 0.0, MASK_LOGIT), run + jnp.sum(eqf, axis=0, keepdims=True)

        def plain(run):
            return jnp.where(s >= thr, 0.0, MASK_LOGIT), run

        bias, run = lax.cond(partial_ties, ranked, plain, run)
        m_rows = [m[h:h + 1, :] for h in range(N_HEADS)]
        l_rows = [l[h:h + 1, :] for h in range(N_HEADS)]
        def logits(sub, h):
            ks = pl.ds(pl.multiple_of(k0 + sub * ATT_SUB, ATT_SUB), ATT_SUB)
            hs = slice(h * HEAD_DIM, (h + 1) * HEAD_DIM)
            return jnp.dot(kb_ref[ks, hs], q_t[hs, :], preferred_element_type=F32)

        order = [(sub, h) for sub in range(tk // ATT_SUB) for h in range(N_HEADS)]
        raws = [logits(*u) for u in order[:QK_AHEAD]]
        pending = None
        for n, (sub, h) in enumerate(order):
            if n + QK_AHEAD < len(order):
                raws.append(logits(*order[n + QK_AHEAD]))
            lg = bias[sub * ATT_SUB:(sub + 1) * ATT_SUB, :] + raws[n]
            m_new = jnp.maximum(m_rows[h], jnp.max(lg, axis=0, keepdims=True))
            p = jnp.exp2(lg - m_new)
            alpha = jnp.exp2(m_rows[h] - m_new)
            l_rows[h] = alpha * l_rows[h] + jnp.sum(p, axis=0, keepdims=True)
            m_rows[h] = m_new
            ks = pl.ds(pl.multiple_of(k0 + sub * ATT_SUB, ATT_SUB), ATT_SUB)
            pv = jnp.dot(vt_ref[h * HEAD_DIM:(h + 1) * HEAD_DIM, ks], p.astype(BF16),
                         preferred_element_type=F32)
            if pending is not None:
                ph, palpha, ppv = pending
                acc_refs[ph][...] = palpha * acc_refs[ph][...] + ppv
            pending = (h, alpha, pv)
        ph, palpha, ppv = pending
        acc_refs[ph][...] = palpha * acc_refs[ph][...] + ppv
        return run, jnp.concatenate(m_rows, axis=0), jnp.concatenate(l_rows, axis=0)

    _, _, l = lax.fori_loop(0, n_kt, attend_tile,
                            (jnp.zeros((1, tq), F32), jnp.full((N_HEADS, tq), M_INIT, F32),
                             jnp.zeros((N_HEADS, tq), F32)))
    out_t = jnp.concatenate([acc_ref[...] / l[h:h + 1, :] for h, acc_ref in enumerate(acc_refs)], axis=0)
    o_ref[...] = out_t.T


def _attention_t(main, kb, vt, tri_lower, *, batch, tq, topk, frames_start, k_lo, k_hi):
    s_len = kb.shape[0] // batch
    n_q = s_len // tq
    assert tq == KEY_TILE and frames_start == KEY_TILE and KEY_TILE % CHUNK == 0
    assert 0 <= k_lo <= KEY_TILE and k_hi == s_len and s_len % KEY_TILE == 0
    kern = functools.partial(_attn_t_kernel, tq=tq, topk=topk, frames_start=frames_start, k_lo=k_lo, k_hi=k_hi)
    return pl.pallas_call(
        kern,
        grid=(batch, n_q),
        in_specs=[
            pl.BlockSpec((tq, ATTN_WIDTH), lambda b, i: (b * n_q + i, COL_Q // ATTN_WIDTH)),
            pl.BlockSpec((tq, IDX_HEADS * IDX_DIM), lambda b, i: (b * n_q + i, COL_QI // (IDX_HEADS * IDX_DIM))),
            pl.BlockSpec((tq, LANES), lambda b, i: (b * n_q + i, COL_KIW // LANES)),
            pl.BlockSpec((s_len, ATTN_WIDTH), lambda b, i: (b, 0)),
            pl.BlockSpec((ATTN_WIDTH, s_len), lambda b, i: (0, b)),
            pl.BlockSpec((s_len, LANES), lambda b, i: (b, COL_KIW // LANES)),
            pl.BlockSpec((KEY_TILE, KEY_TILE), lambda b, i: (0, 0)),
        ],
        out_specs=pl.BlockSpec((tq, ATTN_WIDTH), lambda b, i: (b * n_q + i, 0)),
        out_shape=jax.ShapeDtypeStruct((main.shape[0], ATTN_WIDTH), F32),
        scratch_shapes=[pltpu.VMEM((s_len, tq), F32)] + [pltpu.VMEM((HEAD_DIM, tq), F32)] * N_HEADS,
        compiler_params=_cparams(("arbitrary", "arbitrary")),
        name="attn_t",
    )(main, main, main, kb, vt, main, tri_lower)


def _lru_kernel(xb_ref, cst_ref, h0_ref, cw_ref, cb_ref, wa_ref, ba_ref, wx_ref, bx_ref, lam_ref,
                hb_ref, hlast_ref, xpad_ref, a_ref, b_ref, hs_ref, h_ref, *, batch, tt, n_pad):
    t = pl.program_id(0)
    hist = SUBLANES

    @pl.when(t == 0)
    def _():
        xpad_ref[:, 0:hist, :] = jnp.zeros((batch, hist, LRU_WIDTH), F32)
        xpad_ref[:, hist - (CONV_WIDTH - 1):hist, :] = cst_ref[...]
        h_ref[...] = h0_ref[...]

    xpad_ref[:, hist:hist + tt, :] = xb_ref[...]
    xc = jnp.broadcast_to(cb_ref[...].reshape(1, 1, LRU_WIDTH), (batch, tt, LRU_WIDTH))
    for j in range(CONV_WIDTH):
        off = hist - (CONV_WIDTH - 1) + j
        xc = xc + xpad_ref[:, off:off + tt, :] * cw_ref[j:j + 1, :].reshape(1, 1, LRU_WIDTH)
    xpad_ref[:, 0:hist, :] = xpad_ref[:, tt:tt + hist, :]

    xc = xc.reshape(batch * tt, LRU_WIDTH)
    xcb = xc.astype(BF16)
    r = jax.nn.sigmoid(jnp.dot(xcb, wa_ref[...], preferred_element_type=F32) + ba_ref[...])
    g = jax.nn.sigmoid(jnp.dot(xcb, wx_ref[...], preferred_element_type=F32) + bx_ref[...])
    log_a = -LRU_C * r * jax.nn.softplus(-lam_ref[...])
    a = jnp.exp(log_a)
    b = jnp.sqrt(1.0 - a * a) * (g * xc)
    if n_pad:
        time = t * tt + lax.broadcasted_iota(jnp.int32, (batch * tt, 1), 0) % tt
        b = jnp.where(time >= n_pad, b, 0.0)
    n_slab = LRU_WIDTH // LANES
    for c in range(n_slab):
        a_ref[c] = a[:, c * LANES:(c + 1) * LANES]
        b_ref[c] = b[:, c * LANES:(c + 1) * LANES]

    def group(gi, hs):
        hs = list(hs)
        for j in range(SUBLANES):
            rows = pl.ds(gi * SUBLANES + j, batch, stride=tt)
            for c in range(n_slab):
                hs[c] = a_ref[c, rows, :] * hs[c] + b_ref[c, rows, :]
                hs_ref[c, rows, :] = hs[c]
        return tuple(hs)

    h0 = h_ref[...]
    hs = lax.fori_loop(0, tt // SUBLANES, group, tuple(h0[:, c * LANES:(c + 1) * LANES] for c in range(n_slab)))
    h = jnp.concatenate(hs, axis=1)
    h_ref[...] = h
    hlast_ref[...] = h
    hb_ref[...] = jnp.concatenate([hs_ref[c] for c in range(n_slab)], axis=1).reshape(batch, tt, LRU_WIDTH)


def _lru(main, conv_st, h0, lw, *, batch, tt, n_pad):
    rows = main.shape[0] // batch
    const = lambda shape: pl.BlockSpec(shape, lambda t: (0,) * len(shape))
    hb, h_last = pl.pallas_call(
        functools.partial(_lru_kernel, batch=batch, tt=tt, n_pad=n_pad),
        grid=(rows // tt,),
        in_specs=[
            pl.BlockSpec((batch, tt, LRU_WIDTH), lambda t: (0, t, COL_XB // LRU_WIDTH)),
            const((batch, CONV_WIDTH - 1, LRU_WIDTH)), const((batch, LRU_WIDTH)),
            const((CONV_WIDTH, LRU_WIDTH)), const((1, LRU_WIDTH)),
            const((LRU_WIDTH, LRU_WIDTH)), const((1, LRU_WIDTH)),
            const((LRU_WIDTH, LRU_WIDTH)), const((1, LRU_WIDTH)),
            const((1, LRU_WIDTH)),
        ],
        out_specs=[
            pl.BlockSpec((batch, tt, LRU_WIDTH), lambda t: (0, t, 0)),
            const((batch, LRU_WIDTH)),
        ],
        out_shape=[
            jax.ShapeDtypeStruct((batch, rows, LRU_WIDTH), F32),
            jax.ShapeDtypeStruct((batch, LRU_WIDTH), F32),
        ],
        scratch_shapes=[
            pltpu.VMEM((batch, tt + SUBLANES, LRU_WIDTH), F32),
            pltpu.VMEM((LRU_WIDTH // LANES, batch * tt, LANES), F32),
            pltpu.VMEM((LRU_WIDTH // LANES, batch * tt, LANES), F32),
            pltpu.VMEM((LRU_WIDTH // LANES, batch * tt, LANES), F32),
            pltpu.VMEM((batch, LRU_WIDTH), F32),
        ],
        compiler_params=_cparams(("arbitrary",)),
        name="lru",
    )(main.reshape(batch, rows, N_PACKED), conv_st, h0, lw["conv_w"], lw["conv_b"], lw["wa"], lw["ba"], lw["wx"],
      lw["bx"], lw["lam"])
    return hb.reshape(batch * rows, LRU_WIDTH), h_last


def _pool_kernel(xc_ref, pst_ref, pw_ref, ps_ref, hc_ref, xpad_ref, *, tt, n_pad, n_hist):
    t = pl.program_id(1)
    hist = 2 * SUBLANES

    @pl.when(t == 0)
    def _():
        xpad_ref[0:hist, :] = jnp.zeros((hist, POOL_WIDTH), F32)
        xpad_ref[hist - POOL_HIST:hist, :] = pst_ref[0]

    x = xc_ref[...]
    xpad_ref[hist:hist + tt, :] = x
    row = t * tt + lax.broadcasted_iota(jnp.int32, (tt, 1), 0)
    seen = jnp.maximum(row - n_pad + 1 + n_hist, 1).astype(F32)
    outs = []
    for gi, win in enumerate(POOL_WINDOWS):
        cs = slice(gi * POOL_GROUP_DIM, (gi + 1) * POOL_GROUP_DIM)
        tot = x[:, cs]
        for j in range(1, win):
            tot = tot + xpad_ref[hist - j:hist - j + tt, cs]
        pooled = tot / jnp.minimum(float(win), seen) - x[:, cs]
        outs.append(jnp.dot(pooled.astype(BF16), pw_ref[gi], preferred_element_type=F32))
    hc_ref[...] = jnp.concatenate(outs, axis=1) * ps_ref[...]
    xpad_ref[0:hist, :] = xpad_ref[tt:tt + hist, :]


def _pool(main, pool_st, pw, ps, *, batch, tt, n_pad, n_hist):
    n_t = main.shape[0] // batch // tt
    kern = functools.partial(_pool_kernel, tt=tt, n_pad=n_pad, n_hist=n_hist)
    return pl.pallas_call(
        kern,
        grid=(batch, n_t),
        in_specs=[
            pl.BlockSpec((tt, POOL_WIDTH), lambda b, t: (b * n_t + t, COL_XC // POOL_WIDTH)),
            pl.BlockSpec((1, POOL_HIST, POOL_WIDTH), lambda b, t: (b, 0, 0)),
            pl.BlockSpec((POOL_GROUPS, POOL_GROUP_DIM, POOL_GROUP_DIM), lambda b, t: (0, 0, 0)),
            pl.BlockSpec((1, POOL_WIDTH), lambda b, t: (0, 0)),
        ],
        out_specs=pl.BlockSpec((tt, POOL_WIDTH), lambda b, t: (b * n_t + t, 0)),
        out_shape=jax.ShapeDtypeStruct((main.shape[0], POOL_WIDTH), F32),
        scratch_shapes=[pltpu.VMEM((tt + 2 * SUBLANES, POOL_WIDTH), F32)],
        compiler_params=_cparams(("arbitrary", "arbitrary")),
        name="pool",
    )(main, pool_st, pw, ps)


def _silu(v):
    return v * jax.nn.sigmoid(v)


def _merge_kernel(x_ref, at_ref, ga_ref, hb_ref, gb_ref, hc_ref, gc_ref, gm0_ref, gm1_ref, gm2_ref,
                  wb_ref, wo_ref, fg_ref, o_ref, *, tm, rows_per_seq, n_pad, final):
    ys = (at_ref[...] * _silu(ga_ref[...]), hb_ref[...] * _silu(gb_ref[...]), hc_ref[...] * _silu(gc_ref[...]))
    merged = jnp.zeros((tm, D_MODEL), F32)
    for n, (y, gm_ref) in enumerate(zip(ys, (gm0_ref, gm1_ref, gm2_ref))):
        pb = jnp.dot(y.astype(BF16), wb_ref[n], preferred_element_type=F32)
        merged = merged + jax.nn.sigmoid(gm_ref[...]) * pb
    out = x_ref[...] + jnp.dot(merged.astype(BF16), wo_ref[...], preferred_element_type=F32)
    if n_pad:
        row = (pl.program_id(0) * tm) % rows_per_seq + lax.broadcasted_iota(jnp.int32, (tm, 1), 0)
        out = jnp.where(row >= n_pad, out, 0.0)
    if final:
        ms = jnp.mean(out * out, axis=-1, keepdims=True)
        out = out * lax.rsqrt(ms + RMS_EPS) * fg_ref[...]
    o_ref[...] = out


def _merge(x2d, attn, hb, hc, main, wb, wo, fg, *, rows_per_seq, n_pad, final, skip_tiles=0):
    tm = ROW_TILE
    if skip_tiles:
        assert rows_per_seq % tm == 0 and n_pad <= skip_tiles * tm
        tiles_in = rows_per_seq // tm
        tiles_out = tiles_in - skip_tiles
        m = x2d.shape[0] // tiles_in * tiles_out
        src = lambda i: i // tiles_out * tiles_in + skip_tiles + i % tiles_out
    else:
        m = x2d.shape[0]
        src = lambda i: i
    kern = functools.partial(_merge_kernel, tm=tm, rows_per_seq=rows_per_seq, n_pad=0 if skip_tiles else n_pad,
                             final=final)
    half = lambda col: pl.BlockSpec((tm, BRANCH_WIDTH), lambda i: (src(i), col // BRANCH_WIDTH))
    full = lambda col: pl.BlockSpec((tm, D_MODEL), lambda i: (src(i), col // D_MODEL))
    return pl.pallas_call(
        kern,
        grid=(m // tm,),
        in_specs=[
            full(0), half(0), half(COL_GA), half(0), half(COL_GB), half(0), half(COL_GC),
            full(COL_GM), full(COL_GM + D_MODEL), full(COL_GM + 2 * D_MODEL),
            pl.BlockSpec((N_BRANCH, BRANCH_WIDTH, D_MODEL), lambda i: (0, 0, 0)),
            pl.BlockSpec((D_MODEL, D_MODEL), lambda i: (0, 0)),
            pl.BlockSpec((1, D_MODEL), lambda i: (0, 0)),
        ],
        out_specs=pl.BlockSpec((tm, D_MODEL), lambda i: (i, 0)),
        out_shape=jax.ShapeDtypeStruct((m, D_MODEL), F32),
        compiler_params=_cparams(("arbitrary",)),
        name="merge",
    )(x2d, attn, main, hb, main, hc, main, main, main, main, wb, wo, fg)


def _pack_w_in(w):
    offs = np.cumsum((0,) + COL_SIZES)
    q, k, v, ga, qi, ki, wi, xb, gb, xc, gc, gm = [w[:, offs[j]:offs[j + 1]] for j in range(len(COL_SIZES))]
    pad = jnp.zeros((w.shape[0], LANES - IDX_DIM - IDX_HEADS), w.dtype)
    return jnp.concatenate([q, k, v, ga, xb, gb, xc, gc, gm, qi, ki, wi, pad], axis=1).astype(BF16)


def _block_diag(w):
    nb, c, d = w.shape
    eye = jnp.eye(nb, dtype=w.dtype)
    return (w[:, :, None, :] * eye[:, None, :, None]).reshape(nb * c, nb * d)


def _rope_table(pos):
    half = ROT_DIM // 2
    inv = ROPE_THETA ** (-jnp.arange(0, ROT_DIM, 2, dtype=F32) / ROT_DIM)
    ang = pos.astype(F32)[:, None] * inv[None, :]
    cos, sin = jnp.cos(ang), jnp.sin(ang)
    n = pos.shape[0]
    rest1 = jnp.ones((n, HEAD_DIM - ROT_DIM), F32)
    rest0 = jnp.zeros((n, HEAD_DIM - ROT_DIM), F32)
    zh = jnp.zeros((n, half), F32)
    c = jnp.concatenate([cos, cos, rest1], axis=1)
    s1 = jnp.concatenate([-sin, zh, rest0], axis=1)
    s2 = jnp.concatenate([zh, sin, rest0], axis=1)
    return jnp.concatenate([c, c, s1, s1, s2, s2], axis=1)


def _layer_weights(l, norm_g, w_in, conv_w, conv_b, lru_wa, lru_ba, lru_wx, lru_bx, lru_lambda, pool_w,
                   pool_scale, w_branch_out, w_out):
    row = lambda v: v.reshape(1, -1)
    return dict(
        g=row(norm_g[l]), wp=_pack_w_in(w_in[l]),
        lru=dict(conv_w=conv_w[l], conv_b=row(conv_b[l]), wa=_block_diag(lru_wa[l]).astype(BF16), ba=row(lru_ba[l]),
                 wx=_block_diag(lru_wx[l]).astype(BF16), bx=row(lru_bx[l]), lam=row(lru_lambda[l])),
        pw=pool_w[l].astype(BF16), ps=row(pool_scale[l]),
        wb=w_branch_out[l].astype(BF16), wo=w_out[l].astype(BF16))


def _trunk(x2d, layers, final_g, rope, rope_tiles, *, batch, rows, tq, n_pad, topk, frames_start,
           n_hist, past, conv_st, lru_st, pool_st, out_skip_tiles=0):
    ids = lax.broadcasted_iota(jnp.int32, (KEY_TILE, KEY_TILE), 0), lax.broadcasted_iota(jnp.int32, (KEY_TILE, KEY_TILE), 1)
    tri_upper = (ids[0] < ids[1]).astype(BF16)
    tri_lower = (ids[0] > ids[1]).astype(BF16)
    depth = len(layers)
    news = []
    for l, lw in enumerate(layers):
        main, kb, vb = _proj(x2d, lw["g"], lw["wp"], rope, rope_tiles, past is None)
        if past is None:
            attn = _attention_t(main, kb, vb, tri_lower, batch=batch, tq=tq, topk=topk, frames_start=frames_start,
                                k_lo=n_pad, k_hi=rows)
        else:
            attn = _attention_c(main, *past, l, tri_upper, batch=batch, t_new=rows, topk=topk)
        hb, h_last = _lru(main, conv_st[l], lru_st[l], lw["lru"], batch=batch, tt=tq, n_pad=n_pad)
        hc = _pool(main, pool_st[l], lw["pw"], lw["ps"], batch=batch, tt=tq, n_pad=n_pad, n_hist=n_hist)
        last = l == depth - 1
        x2d = _merge(x2d, attn, hb, hc, main, lw["wb"], lw["wo"], final_g, rows_per_seq=rows, n_pad=n_pad,
                     final=last, skip_tiles=out_skip_tiles if last else 0)
        m3 = main.reshape(batch, rows, N_PACKED)
        real = rows - n_pad
        k_new = m3[:, n_pad:, COL_K:COL_K + ATTN_WIDTH].reshape(batch, real, N_HEADS, HEAD_DIM)
        v_new = m3[:, n_pad:, COL_V:COL_V + ATTN_WIDTH].reshape(batch, real, N_HEADS, HEAD_DIM)
        ki_new = m3[:, n_pad:, COL_KIW:COL_KIW + IDX_DIM]
        xb_seq = jnp.concatenate([conv_st[l], m3[:, n_pad:, COL_XB:COL_XB + LRU_WIDTH][:, -(CONV_WIDTH - 1):]], axis=1)
        xc_seq = jnp.concatenate([pool_st[l], m3[:, n_pad:, COL_XC:COL_XC + POOL_WIDTH][:, -POOL_HIST:]], axis=1)
        news.append((k_new, v_new, ki_new, xb_seq[:, -(CONV_WIDTH - 1):], h_last.reshape(batch, LRU_WIDTH),
                     xc_seq[:, -POOL_HIST:]))
    stacked = [jnp.stack([n[j] for n in news]) for j in range(6)]
    return x2d, stacked


def kernel(x_prompt, x_sample, cache_k, cache_v, cache_kidx, state_conv, state_lru, state_pool, meta_tokens,
           norm_g, w_in, conv_w, conv_b, lru_wa, lru_ba, lru_wx, lru_bx, lru_lambda, pool_w, pool_scale,
           w_branch_out, w_out, final_norm_g):
    depth = w_in.shape[0]
    layers = [_layer_weights(l, norm_g, w_in, conv_w, conv_b, lru_wa, lru_ba, lru_wx, lru_bx, lru_lambda, pool_w,
                             pool_scale, w_branch_out, w_out) for l in range(depth)]
    final_g = final_norm_g.reshape(1, D_MODEL)
    dt = x_prompt.dtype

    bp, seq, _ = x_prompt.shape
    assert seq % ROW_TILE == 0 and N_META <= ROW_TILE
    n_pad = ROW_TILE - N_META
    rows_p = ROW_TILE + seq
    x0 = jnp.concatenate([jnp.zeros((bp, n_pad, D_MODEL), dt),
                          jnp.broadcast_to(meta_tokens.astype(dt)[None], (bp, N_META, D_MODEL)), x_prompt], axis=1)
    pos_p = jnp.maximum(jnp.arange(rows_p, dtype=jnp.int32) - n_pad, 0)
    zeros = lambda *s: jnp.zeros((depth, bp) + s, dt)
    y_full, (k_p, v_p, ki_p, conv_p, lru_p, pool_p) = _trunk(
        x0.reshape(bp * rows_p, D_MODEL), layers, final_g, _rope_table(pos_p), rows_p // ROW_TILE,
        batch=bp, rows=rows_p, tq=ROW_TILE, n_pad=n_pad, topk=min(TOPK_MAX, seq // 4),
        frames_start=ROW_TILE, n_hist=0, past=None, conv_st=zeros(CONV_WIDTH - 1, LRU_WIDTH),
        lru_st=zeros(LRU_WIDTH), pool_st=zeros(POOL_HIST, POOL_WIDTH), out_skip_tiles=1)
    y_prompt = y_full.reshape(bp, seq, D_MODEL)

    bs, t1, _ = x_sample.shape
    p_len = cache_k.shape[2]
    assert ROW_TILE % t1 == 0 and (bs * t1) % ROW_TILE == 0 and t1 % SUBLANES == 0
    pos_s = p_len + jnp.arange(t1, dtype=jnp.int32)
    rope_s = jnp.tile(_rope_table(pos_s), (ROW_TILE // t1, 1))
    past = (jnp.transpose(cache_k, (0, 1, 3, 4, 2)).reshape(depth * bs, ATTN_WIDTH, p_len),
            jnp.transpose(cache_v, (0, 1, 3, 4, 2)).reshape(depth * bs, ATTN_WIDTH, p_len),
            jnp.transpose(cache_kidx, (0, 1, 3, 2)).reshape(depth * bs, IDX_DIM, p_len))
    y_s, (k_s, v_s, ki_s, conv_s, lru_s, pool_s) = _trunk(
        x_sample.reshape(bs * t1, D_MODEL), layers, final_g, rope_s, 1,
        batch=bs, rows=t1, tq=t1, n_pad=0, topk=min(TOPK_MAX, (p_len + t1) // 4),
        frames_start=0, n_hist=p_len, past=past, conv_st=state_conv, lru_st=state_lru, pool_st=state_pool)
    y_sample = y_s.reshape(bs, t1, D_MODEL)
    return (y_prompt, y_sample, k_p, v_p, ki_p, conv_p, lru_p, pool_p, k_s, v_s, ki_s, conv_s, lru_s, pool_s)
```
